```python
import math
import jax, jax.numpy as jnp
from jax import lax
import numpy as np

D_MODEL = 2048
BATCH = 4
SEQ = 8192
DEPTH = 2

HEAD_DIM = 128
N_BRANCHES = 4
BRANCH_WIDTH = 512
LRU_WIDTH = 512
LRU_BLOCKS = 8
LRU_BLOCK_DIM = LRU_WIDTH // LRU_BLOCKS
LRU_CONV = 4
LRU_C = 8.0
DIFF_HEADS = 4
DIFF_QK_DIM = 64
DIFF_V_DIM = 128
DIFF_ROPE_DIM = DIFF_QK_DIM // 4
MLA_HEADS = 4
MLA_Q_RANK = 512
MLA_KV_RANK = 256
MLA_NOPE_DIM = 128
MLA_ROPE_DIM = 64
MLA_V_DIM = 128
MLA_QK_DIM = MLA_NOPE_DIM + MLA_ROPE_DIM
GQA_Q_HEADS = 4
GQA_KV_HEADS = 2
GQA_GROUP = GQA_Q_HEADS // GQA_KV_HEADS
GRID_W = 64
ROPE_THETA = 500000.0
AXIAL_THETA = 10000.0
Q_BLOCK = 128
EPS = 1e-6
FFN_DIM = 256 * ((8 * D_MODEL // 3 + 255) // 256)
N_EXPERTS = 8
TOP_K = 2
EXPERT_DIM = 7 * D_MODEL // 2
N_DENSE = (DEPTH + 1) // 2
N_MOE = DEPTH // 2
IN_SPLIT_WIDTHS = (
    LRU_WIDTH, LRU_WIDTH,
    DIFF_HEADS * 2 * DIFF_QK_DIM, DIFF_HEADS * 2 * DIFF_QK_DIM, DIFF_HEADS * DIFF_V_DIM,
    MLA_Q_RANK, MLA_KV_RANK, MLA_ROPE_DIM,
    GQA_Q_HEADS * HEAD_DIM, GQA_KV_HEADS * HEAD_DIM, GQA_KV_HEADS * HEAD_DIM,
    N_BRANCHES * D_MODEL,
)
IN_COLS = sum(IN_SPLIT_WIDTHS)

kernel_name = 'hybrid_gated_lru_diff_mla_axial_moe'


def rmsnorm(x, g):
    xf = x.astype(jnp.float32)
    y = xf * lax.rsqrt(jnp.mean(xf * xf, axis=-1, keepdims=True) + EPS)
    return (y * g.astype(jnp.float32)).astype(x.dtype)


def rope_tables(pos, dim, theta):
    inv = jnp.power(jnp.float32(theta), -jnp.arange(0, dim, 2, dtype=jnp.float32) / dim)
    ang = pos.astype(jnp.float32)[:, None] * inv[None, :]
    return jnp.cos(ang), jnp.sin(ang)


def apply_rope(x, cos, sin):
    half = x.shape[-1] // 2
    xf = x.astype(jnp.float32)
    x1, x2 = xf[..., :half], xf[..., half:]
    return jnp.concatenate([x1 * cos - x2 * sin, x2 * cos + x1 * sin], axis=-1).astype(x.dtype)


def _to_blocks(a):
    *lead, s, d = a.shape
    a = a.reshape(*lead, s // Q_BLOCK, Q_BLOCK, d)
    return jnp.moveaxis(a, -3, 0)


def _from_blocks(o):
    nb = o.shape[0]
    o = jnp.moveaxis(o, 0, -3)
    return o.reshape(*o.shape[:-3], nb * Q_BLOCK, o.shape[-1])


def sweep_query_blocks(block_fn, *q_arrays):
    out = lax.map(lambda qs: block_fn(*qs), tuple(_to_blocks(a) for a in q_arrays))
    return _from_blocks(out)


def dense_attention(q, k, v, scale):
    def block(qb):
        s = jnp.einsum('bhgqd,bhkd->bhgqk', qb, k).astype(jnp.float32) * scale
        p = jax.nn.softmax(s, axis=-1).astype(v.dtype)
        return jnp.einsum('bhgqk,bhkd->bhgqd', p, v)
    return sweep_query_blocks(block, q)


def _lin_combine(left, right):
    a1, b1 = left
    a2, b2 = right
    return a1 * a2, a2 * b1 + b2


def rg_lru_branch(u_gate, u_x, conv_w, conv_b, gate_w, gate_b, lam):
    b_, s_, w_ = u_x.shape
    xc = lax.conv_general_dilated(
        u_x, conv_w[:, None, :], window_strides=(1,), padding=((2, 1),),
        dimension_numbers=('NWC', 'WIO', 'NWC'), feature_group_count=w_) + conv_b
    xb = xc.reshape(b_, s_, LRU_BLOCKS, LRU_BLOCK_DIM)
    gates = jnp.einsum('bsnk,dgnkj->dgbsnj', xb, gate_w).reshape(2, 2, b_, s_, w_)
    gates = jax.nn.sigmoid((gates + gate_b[:, :, None, None, :]).astype(jnp.float32))
    r, i = gates[:, 0], gates[:, 1]
    log_a = -LRU_C * r * jax.nn.softplus(-lam.astype(jnp.float32))[:, None, None, :]
    a = jnp.exp(log_a)
    b = jnp.sqrt(-jnp.expm1(2.0 * log_a)) * (i * xc.astype(jnp.float32)[None])
    h_fwd = lax.associative_scan(_lin_combine, (a[0], b[0]), axis=1)[1]
    h_bwd = lax.associative_scan(_lin_combine, (a[1], b[1]), axis=1, reverse=True)[1]
    return ((h_fwd + h_bwd) * jax.nn.gelu(u_gate.astype(jnp.float32))).astype(u_x.dtype)


def diff_attention_branch(u_q, u_k, u_v, q_norm, k_norm, lam, out_norm, lambda_init, cos, sin):
    b_, s_, _ = u_q.shape
    q = u_q.reshape(b_, s_, DIFF_HEADS, 2, DIFF_QK_DIM).transpose(3, 0, 2, 1, 4)
    k = u_k.reshape(b_, s_, DIFF_HEADS, 2, DIFF_QK_DIM).transpose(3, 0, 2, 1, 4)
    v = u_v.reshape(b_, s_, DIFF_HEADS, DIFF_V_DIM).transpose(0, 2, 1, 3)
    q, k = rmsnorm(q, q_norm), rmsnorm(k, k_norm)
    q = jnp.concatenate([apply_rope(q[..., :DIFF_ROPE_DIM], cos, sin), q[..., DIFF_ROPE_DIM:]], axis=-1)
    k = jnp.concatenate([apply_rope(k[..., :DIFF_ROPE_DIM], cos, sin), k[..., DIFF_ROPE_DIM:]], axis=-1)
    lf = lam.astype(jnp.float32)
    lam_full = jnp.exp(jnp.sum(lf[0] * lf[1])) - jnp.exp(jnp.sum(lf[2] * lf[3])) + lambda_init
    scale = DIFF_QK_DIM ** -0.5
    k1, k2 = k[0], k[1]

    def block(q1b, q2b):
        s1 = jnp.einsum('bhqd,bhkd->bhqk', q1b, k1).astype(jnp.float32) * scale
        s2 = jnp.einsum('bhqd,bhkd->bhqk', q2b, k2).astype(jnp.float32) * scale
        p = jax.nn.softmax(s1, axis=-1) - lam_full * jax.nn.softmax(s2, axis=-1)
        return jnp.einsum('bhqk,bhkd->bhqd', p.astype(v.dtype), v)

    o = sweep_query_blocks(block, q[0], q[1])
    o = rmsnorm(o, out_norm) * (1.0 - lambda_init)
    return o.transpose(0, 2, 1, 3).reshape(b_, s_, DIFF_HEADS * DIFF_V_DIM)


def mla_branch(c_q, c_kv, k_rope, cq_norm, ckv_norm, w_uq, w_ukv, q_norm, k_norm, cos, sin):
    b_, s_, _ = c_q.shape
    q = (rmsnorm(c_q, cq_norm) @ w_uq).reshape(b_, s_, MLA_HEADS, MLA_QK_DIM).transpose(0, 2, 1, 3)
    kv = (rmsnorm(c_kv, ckv_norm) @ w_ukv).reshape(b_, s_, MLA_HEADS, MLA_NOPE_DIM + MLA_V_DIM).transpose(0, 2, 1, 3)
    k_nope, v = kv[..., :MLA_NOPE_DIM], kv[..., MLA_NOPE_DIM:]
    k_r = jnp.broadcast_to(k_rope[:, None], (b_, MLA_HEADS, s_, MLA_ROPE_DIM))
    k = jnp.concatenate([k_nope, k_r], axis=-1)
    q, k = rmsnorm(q, q_norm), rmsnorm(k, k_norm)
    q = jnp.concatenate([q[..., :MLA_NOPE_DIM], apply_rope(q[..., MLA_NOPE_DIM:], cos, sin)], axis=-1)
    k = jnp.concatenate([k[..., :MLA_NOPE_DIM], apply_rope(k[..., MLA_NOPE_DIM:], cos, sin)], axis=-1)
    o = dense_attention(q[:, :, None], k, v, MLA_QK_DIM ** -0.5)[:, :, 0]
    return o.transpose(0, 2, 1, 3).reshape(b_, s_, MLA_HEADS * MLA_V_DIM)


def axial_rope(x, cos_r, sin_r, cos_c, sin_c):
    half = x.shape[-1] // 2
    return jnp.concatenate([apply_rope(x[..., :half], cos_r, sin_r),
                            apply_rope(x[..., half:], cos_c, sin_c)], axis=-1)


def gqa_axial_branch(u_q, u_k, u_v, q_norm, k_norm, cos_r, sin_r, cos_c, sin_c):
    b_, s_, _ = u_q.shape
    q = u_q.reshape(b_, s_, GQA_KV_HEADS, GQA_GROUP, HEAD_DIM).transpose(0, 2, 3, 1, 4)
    k = u_k.reshape(b_, s_, GQA_KV_HEADS, HEAD_DIM).transpose(0, 2, 1, 3)
    v = u_v.reshape(b_, s_, GQA_KV_HEADS, HEAD_DIM).transpose(0, 2, 1, 3)
    q = axial_rope(rmsnorm(q, q_norm), cos_r, sin_r, cos_c, sin_c)
    k = axial_rope(rmsnorm(k, k_norm), cos_r, sin_r, cos_c, sin_c)
    o = dense_attention(q, k, v, HEAD_DIM ** -0.5)
    return o.transpose(0, 3, 1, 2, 4).reshape(b_, s_, GQA_Q_HEADS * HEAD_DIM)


def swiglu(x, w_gate_up, w_down):
    g, u = jnp.split(x @ w_gate_up, 2, axis=-1)
    return (jax.nn.silu(g) * u) @ w_down


def moe_ffn(x, router, w_gate_up, w_down):
    logits = (x @ router).astype(jnp.float32)
    top_v, top_i = lax.top_k(logits, TOP_K)
    top_w = jax.nn.softmax(top_v, axis=-1)
    gates = jnp.sum(jax.nn.one_hot(top_i, N_EXPERTS, dtype=jnp.float32) * top_w[..., None], axis=-2)
    out = jnp.zeros_like(x)
    for e in range(N_EXPERTS):
        out = out + gates[..., e:e + 1].astype(x.dtype) * swiglu(x, w_gate_up[e], w_down[e])
    return out


def setup_inputs(seed: int = 0) -> dict:
    key = jax.random.key(seed)
    ks = jax.random.split(key, 32)
    f32 = jnp.float32

    def nrm(k, shape, scale):
        return jax.random.normal(k, shape, f32) * scale

    def gain(k, shape):
        return 1.0 + 0.02 * jax.random.normal(k, shape, f32)

    u = jax.random.uniform(ks[6], (DEPTH, 2, LRU_WIDTH), f32, 0.9, 0.999)
    p = u ** (1.0 / LRU_C)
    lru_lambda = jnp.log(p) - jnp.log1p(-p)
    return {
        'x': nrm(ks[0], (BATCH, SEQ, D_MODEL), 1.0),
        'norm_mix': gain(ks[1], (DEPTH, D_MODEL)),
        'w_in': nrm(ks[2], (DEPTH, D_MODEL, IN_COLS), D_MODEL ** -0.5),
        'lru_conv_w': nrm(ks[3], (DEPTH, LRU_CONV, LRU_WIDTH), LRU_CONV ** -0.5),
        'lru_conv_b': nrm(ks[4], (DEPTH, LRU_WIDTH), 0.01),
        'lru_gate_w': nrm(ks[5], (DEPTH, 2, 2, LRU_BLOCKS, LRU_BLOCK_DIM, LRU_BLOCK_DIM), LRU_BLOCK_DIM ** -0.5),
        'lru_gate_b': nrm(ks[7], (DEPTH, 2, 2, LRU_WIDTH), 0.01),
        'lru_lambda': lru_lambda,
        'diff_q_norm': gain(ks[8], (DEPTH, DIFF_QK_DIM)),
        'diff_k_norm': gain(ks[9], (DEPTH, DIFF_QK_DIM)),
        'diff_lambda': nrm(ks[10], (DEPTH, 4, DIFF_QK_DIM), 0.1),
        'diff_out_norm': gain(ks[11], (DEPTH, DIFF_V_DIM)),
        'mla_cq_norm': gain(ks[12], (DEPTH, MLA_Q_RANK)),
        'mla_ckv_norm': gain(ks[13], (DEPTH, MLA_KV_RANK)),
        'mla_w_uq': nrm(ks[14], (DEPTH, MLA_Q_RANK, MLA_HEADS * MLA_QK_DIM), MLA_Q_RANK ** -0.5),
        'mla_w_ukv': nrm(ks[15], (DEPTH, MLA_KV_RANK, MLA_HEADS * (MLA_NOPE_DIM + MLA_V_DIM)), MLA_KV_RANK ** -0.5),
        'mla_q_norm': gain(ks[16], (DEPTH, MLA_QK_DIM)),
        'mla_k_norm': gain(ks[17], (DEPTH, MLA_QK_DIM)),
        'gqa_q_norm': gain(ks[18], (DEPTH, HEAD_DIM)),
        'gqa_k_norm': gain(ks[19], (DEPTH, HEAD_DIM)),
        'w_branch': nrm(ks[20], (DEPTH, N_BRANCHES, BRANCH_WIDTH, D_MODEL), BRANCH_WIDTH ** -0.5),
        'w_out': nrm(ks[21], (DEPTH, D_MODEL, D_MODEL), D_MODEL ** -0.5),
        'norm_ffn': gain(ks[22], (DEPTH, D_MODEL)),
        'ffn_w_gate_up': nrm(ks[23], (N_DENSE, D_MODEL, 2 * FFN_DIM), D_MODEL ** -0.5),
        'ffn_w_down': nrm(ks[24], (N_DENSE, FFN_DIM, D_MODEL), FFN_DIM ** -0.5),
        'moe_router': nrm(ks[25], (N_MOE, D_MODEL, N_EXPERTS), D_MODEL ** -0.5),
        'moe_w_gate_up': nrm(ks[26], (N_MOE, N_EXPERTS, D_MODEL, 2 * EXPERT_DIM), D_MODEL ** -0.5),
        'moe_w_down': nrm(ks[27], (N_MOE, N_EXPERTS, EXPERT_DIM, D_MODEL), EXPERT_DIM ** -0.5),
    }


def reference(x, norm_mix, w_in, lru_conv_w, lru_conv_b, lru_gate_w, lru_gate_b, lru_lambda,
              diff_q_norm, diff_k_norm, diff_lambda, diff_out_norm,
              mla_cq_norm, mla_ckv_norm, mla_w_uq, mla_w_ukv, mla_q_norm, mla_k_norm,
              gqa_q_norm, gqa_k_norm, w_branch, w_out, norm_ffn,
              ffn_w_gate_up, ffn_w_down, moe_router, moe_w_gate_up, moe_w_down):
    b_, s_, d_ = x.shape
    rows = s_ // GRID_W
    pos = jnp.arange(s_, dtype=jnp.int32)
    row_pos = jnp.broadcast_to(jnp.arange(rows, dtype=jnp.int32)[:, None], (rows, GRID_W)).reshape(-1)
    col_pos = jnp.broadcast_to(jnp.arange(GRID_W, dtype=jnp.int32)[None, :], (rows, GRID_W)).reshape(-1)
    cos_p, sin_p = rope_tables(pos, DIFF_ROPE_DIM, ROPE_THETA)
    cos_m, sin_m = rope_tables(pos, MLA_ROPE_DIM, ROPE_THETA)
    cos_r, sin_r = rope_tables(row_pos, HEAD_DIM // 2, AXIAL_THETA)
    cos_c, sin_c = rope_tables(col_pos, HEAD_DIM // 2, AXIAL_THETA)
    split_points = [int(p) for p in np.cumsum(IN_SPLIT_WIDTHS)[:-1]]

    for l in range(DEPTH):
        xn = rmsnorm(x, norm_mix[l])
        (a_gate, a_x, b_q, b_k, b_v, c_q, c_kv, c_kr,
         d_q, d_k, d_v, gate_logits) = jnp.split(xn @ w_in[l], split_points, axis=-1)

        y_a = rg_lru_branch(a_gate, a_x, lru_conv_w[l], lru_conv_b[l], lru_gate_w[l], lru_gate_b[l], lru_lambda[l])
        lambda_init = 0.8 - 0.6 * math.exp(-0.3 * l)
        y_b = diff_attention_branch(b_q, b_k, b_v, diff_q_norm[l], diff_k_norm[l], diff_lambda[l],
                                    diff_out_norm[l], lambda_init, cos_p, sin_p)
        y_c = mla_branch(c_q, c_kv, c_kr, mla_cq_norm[l], mla_ckv_norm[l], mla_w_uq[l], mla_w_ukv[l],
                         mla_q_norm[l], mla_k_norm[l], cos_m, sin_m)
        y_d = gqa_axial_branch(d_q, d_k, d_v, gqa_q_norm[l], gqa_k_norm[l], cos_r, sin_r, cos_c, sin_c)

        gates = jax.nn.sigmoid(gate_logits.astype(jnp.float32)).astype(x.dtype).reshape(b_, s_, N_BRANCHES, d_)
        branches = (y_a, y_b, y_c, y_d)
        merged = jnp.zeros_like(x)
        for i in range(N_BRANCHES):
            merged = merged + gates[:, :, i] * (branches[i] @ w_branch[l, i])
        x = x + merged @ w_out[l]

        xn = rmsnorm(x, norm_ffn[l])
        if l % 2 == 0:
            x = x + swiglu(xn, ffn_w_gate_up[l // 2], ffn_w_down[l // 2])
        else:
            x = x + moe_ffn(xn, moe_router[l // 2], moe_w_gate_up[l // 2], moe_w_down[l // 2])
    return x
```

```python
import functools
import math

import jax
import jax.numpy as jnp
from jax import lax
from jax.experimental import pallas as pl
from jax.experimental.pallas import tpu as pltpu

F32 = jnp.float32
BF16 = jnp.bfloat16
EPS = 1e-6
LANES = 128

LRU_BLOCK_DIM = 64
LRU_C = 8.0
DIFF_HEADS = 4
DIFF_QK_DIM = 64
DIFF_ROPE_DIM = 16
MLA_HEADS = 4
MLA_NOPE_DIM = 128
MLA_ROPE_DIM = 64
MLA_QK_DIM = MLA_NOPE_DIM + MLA_ROPE_DIM
MLA_PAD_DIM = 256
MLA_V_DIM = 128
GQA_Q_HEADS = 4
GQA_KV_HEADS = 2
HEAD_DIM = 128
GRID_W = 64
ROPE_THETA = 500000.0
AXIAL_THETA = 10000.0
TOP_K = 2

VMEM_LIMIT = 56 * 1024 * 1024


def _params(*sem):
    return pltpu.CompilerParams(dimension_semantics=sem, vmem_limit_bytes=VMEM_LIMIT)


def _tile(dim, pref):
    return pref if dim % pref == 0 else dim


def _sigmoid(x):
    return 1.0 / (1.0 + jnp.exp(-x))


def _dot(a, b):
    return jnp.dot(a, b, preferred_element_type=F32)


def _rms(x, g):
    ms = jnp.mean(x * x, axis=-1, keepdims=True)
    return x * lax.rsqrt(ms + EPS) * g


def _norm_proj_body(x_ref, g_ref, w_ref, o_ref, xn_ref, xn_scr):
    @pl.when(pl.program_id(1) == 0)
    def _():
        xn = _rms(x_ref[...], g_ref[...]).astype(BF16)
        xn_scr[...] = xn
        xn_ref[...] = xn

    o_ref[...] = _dot(xn_scr[...], w_ref[...]).astype(o_ref.dtype)


def norm_proj(x, g, w, tm, tn):
    m, d = x.shape
    n = w.shape[1]
    return pl.pallas_call(
        _norm_proj_body,
        grid=(m // tm, n // tn),
        in_specs=[pl.BlockSpec((tm, d), lambda i, j: (i, 0)),
                  pl.BlockSpec((1, d), lambda i, j: (0, 0)),
                  pl.BlockSpec((d, tn), lambda i, j: (0, j))],
        out_specs=[pl.BlockSpec((tm, tn), lambda i, j: (i, j)),
                   pl.BlockSpec((tm, d), lambda i, j: (i, 0))],
        out_shape=[jax.ShapeDtypeStruct((m, n), BF16), jax.ShapeDtypeStruct((m, d), BF16)],
        scratch_shapes=[pltpu.VMEM((tm, d), BF16)],
        compiler_params=_params("parallel", "arbitrary"),
        name="norm_proj",
    )(x, g, w)


def _norm_swiglu_body(x_ref, g_ref, wg_ref, wu_ref, o_ref, xn_scr):
    @pl.when(pl.program_id(1) == 0)
    def _():
        xn_scr[...] = _rms(x_ref[...], g_ref[...]).astype(BF16)

    xn = xn_scr[...]
    gate = _dot(xn, wg_ref[...])
    up = _dot(xn, wu_ref[...])
    o_ref[...] = (gate * _sigmoid(gate) * up).astype(o_ref.dtype)


def norm_swiglu(x, g, w_gate_up, tm, tf):
    m, d = x.shape
    f = w_gate_up.shape[1] // 2
    nf = f // tf
    return pl.pallas_call(
        _norm_swiglu_body,
        grid=(m // tm, nf),
        in_specs=[pl.BlockSpec((tm, d), lambda i, j: (i, 0)),
                  pl.BlockSpec((1, d), lambda i, j: (0, 0)),
                  pl.BlockSpec((d, tf), lambda i, j: (0, j)),
                  pl.BlockSpec((d, tf), lambda i, j: (0, nf + j))],
        out_specs=pl.BlockSpec((tm, tf), lambda i, j: (i, j)),
        out_shape=jax.ShapeDtypeStruct((m, f), BF16),
        scratch_shapes=[pltpu.VMEM((tm, d), BF16)],
        compiler_params=_params("parallel", "arbitrary"),
        name="norm_swiglu",
    )(x, g, w_gate_up, w_gate_up)


def _mm_residual_body(a_ref, b_ref, r_ref, o_ref):
    o_ref[...] = r_ref[...] + _dot(a_ref[...], b_ref[...])


def mm_residual(a, b, res, tm, tn):
    m, k = a.shape
    n = b.shape[1]
    return pl.pallas_call(
        _mm_residual_body,
        grid=(m // tm, n // tn),
        in_specs=[pl.BlockSpec((tm, k), lambda i, j: (i, 0)),
                  pl.BlockSpec((k, tn), lambda i, j: (0, j)),
                  pl.BlockSpec((tm, tn), lambda i, j: (i, j))],
        out_specs=pl.BlockSpec((tm, tn), lambda i, j: (i, j)),
        out_shape=jax.ShapeDtypeStruct((m, n), F32),
        compiler_params=_params("parallel", "arbitrary"),
        name="mm_residual",
    )(a, b, res)


def _lru_body(ug_ref, ux_ref, cw_ref, cb_ref, gw_ref, gb_ref, lam_ref, o_ref, xpad, hf, *, chunk):
    s = ux_ref.shape[1]
    t = chunk
    nc = s // t
    halo = 8
    xpad[0:halo, :] = jnp.zeros((halo, LANES), F32)
    xpad[s + halo:s + 2 * halo, :] = jnp.zeros((halo, LANES), F32)
    xpad[halo:s + halo, :] = ux_ref[0].astype(F32)

    z = -lam_ref[...]
    softplus = jnp.maximum(z, 0.0) + jnp.log(1.0 + jnp.exp(-jnp.abs(z)))
    decay = -LRU_C * softplus
    row = lax.broadcasted_iota(jnp.int32, (t, LANES), 0)

    def coeffs(c, d):
        ext = xpad[pl.ds(pl.multiple_of(c * t, 8), t + 2 * halo), :]
        xc = cb_ref[...] + sum(cw_ref[j:j + 1, :] * ext[halo - 2 + j:halo - 2 + j + t] for j in range(4))
        xcb = xc.astype(BF16)
        r = _sigmoid(_dot(xcb, gw_ref[0, 2 * d]) + gb_ref[2 * d:2 * d + 1, :])
        i = _sigmoid(_dot(xcb, gw_ref[0, 2 * d + 1]) + gb_ref[2 * d + 1:2 * d + 2, :])
        log_a = decay[d:d + 1, :] * r
        a = jnp.exp(log_a)
        b = jnp.sqrt(1.0 - jnp.exp(2.0 * log_a)) * (i * xc)
        return a, b

    def scan(a, b, reverse):
        k = 1
        while k < t:
            if reverse:
                keep, shift = row < t - k, t - k
            else:
                keep, shift = row >= k, k
            a_s = jnp.where(keep, pltpu.roll(a, shift, 0), 1.0)
            b_s = jnp.where(keep, pltpu.roll(b, shift, 0), 0.0)
            b = a * b_s + b
            a = a * a_s
            k *= 2
        return a, b

    def fwd(c, carry):
        a, b = scan(*coeffs(c, 0), reverse=False)
        h = b + a * carry
        hf[pl.ds(pl.multiple_of(c * t, 8), t), :] = h
        return h[t - 1:t, :]

    lax.fori_loop(0, nc, fwd, jnp.zeros((1, LANES), F32))

    def bwd(i, carry):
        c = nc - 1 - i
        a, b = scan(*coeffs(c, 1), reverse=True)
        h = b + a * carry
        rows = pl.ds(pl.multiple_of(c * t, 16), t)
        g = ug_ref[0, rows, :].astype(F32)
        gelu = 0.5 * g * (1.0 + jnp.tanh(0.7978845608028654 * (g + 0.044715 * (g * g * g))))
        o_ref[0, rows, :] = ((hf[rows, :] + h) * gelu).astype(o_ref.dtype)
        return h[0:1, :]

    lax.fori_loop(0, nc, bwd, jnp.zeros((1, LANES), F32))


def lru_mixer(u, conv_w, conv_b, gate_w_bd, gate_b, lam, gate_col, x_col):
    b, s, _ = u.shape
    w = conv_w.shape[1]
    chunk = _tile(s, 64)
    return pl.pallas_call(
        functools.partial(_lru_body, chunk=chunk),
        grid=(b, w // LANES),
        in_specs=[pl.BlockSpec((1, s, LANES), lambda i, c: (i, 0, gate_col + c)),
                  pl.BlockSpec((1, s, LANES), lambda i, c: (i, 0, x_col + c)),
                  pl.BlockSpec((4, LANES), lambda i, c: (0, c)),
                  pl.BlockSpec((1, LANES), lambda i, c: (0, c)),
                  pl.BlockSpec((1, 4, LANES, LANES), lambda i, c: (c, 0, 0, 0)),
                  pl.BlockSpec((4, LANES), lambda i, c: (0, c)),
                  pl.BlockSpec((2, LANES), lambda i, c: (0, c))],
        out_specs=pl.BlockSpec((1, s, LANES), lambda i, c: (i, 0, c)),
        out_shape=jax.ShapeDtypeStruct((b, s, w), BF16),
        scratch_shapes=[pltpu.VMEM((s + 16, LANES), F32), pltpu.VMEM((s, LANES), F32)],
        compiler_params=_params("parallel", "parallel"),
        name="lru_mixer",
    )(u, u, conv_w, conv_b, gate_w_bd, gate_b, lam)


def _rope(y, c_ref, s1_ref, s2_ref, shift):
    w = y.shape[-1]
    return y * c_ref[...] + pltpu.roll(y, w - shift, 1) * s1_ref[...] + pltpu.roll(y, shift, 1) * s2_ref[...]


def _diff_prep_body(q_ref, k_ref, qg_ref, kg_ref, c_ref, s1_ref, s2_ref, qo_ref, ko_ref):
    lane = lax.broadcasted_iota(jnp.int32, (1, LANES), 1)
    low = lane < DIFF_QK_DIM
    scale = DIFF_QK_DIM ** -0.5
    nh = DIFF_HEADS

    def norm_rope(x, g_ref):
        x2 = x * x
        ss_lo = jnp.sum(jnp.where(low, x2, 0.0), axis=-1, keepdims=True)
        ss_hi = jnp.sum(jnp.where(low, 0.0, x2), axis=-1, keepdims=True)
        ms = jnp.where(low, ss_lo, ss_hi) * (1.0 / DIFF_QK_DIM)
        y = x * lax.rsqrt(ms + EPS) * g_ref[...]
        return _rope(y, c_ref, s1_ref, s2_ref, DIFF_ROPE_DIM // 2)

    for h in range(nh):
        cols = slice(h * LANES, (h + 1) * LANES)
        q = norm_rope(q_ref[0, :, cols].astype(F32), qg_ref) * scale
        qo_ref[0, :, cols] = jnp.where(low, q, 0.0).astype(BF16)
        qo_ref[0, :, nh * LANES + h * LANES:nh * LANES + (h + 1) * LANES] = jnp.where(low, 0.0, q).astype(BF16)
        ko_ref[0, :, cols] = norm_rope(k_ref[0, :, cols].astype(F32), kg_ref).astype(BF16)


def diff_prep(u, q_gain, k_gain, tables, q_col, k_col, ts):
    b, s, _ = u.shape
    w = DIFF_HEADS * LANES
    tab = pl.BlockSpec((ts, LANES), lambda i, j: (j, 0))
    vec = pl.BlockSpec((1, LANES), lambda i, j: (0, 0))
    return pl.pallas_call(
        _diff_prep_body,
        grid=(b, s // ts),
        in_specs=[pl.BlockSpec((1, ts, w), lambda i, j: (i, j, q_col)),
                  pl.BlockSpec((1, ts, w), lambda i, j: (i, j, k_col)),
                  vec, vec, tab, tab, tab],
        out_specs=[pl.BlockSpec((1, ts, 2 * w), lambda i, j: (i, j, 0)),
                   pl.BlockSpec((1, ts, w), lambda i, j: (i, j, 0))],
        out_shape=[jax.ShapeDtypeStruct((b, s, 2 * w), BF16), jax.ShapeDtypeStruct((b, s, w), BF16)],
        compiler_params=_params("parallel", "parallel"),
        name="diff_prep",
    )(u, u, q_gain, k_gain, *tables)


def _gqa_prep_body(q_ref, k_ref, qg_ref, kg_ref, c_ref, s1_ref, s2_ref, qo_ref, ko_ref):
    scale = HEAD_DIM ** -0.5
    for h in range(GQA_Q_HEADS):
        cols = slice(h * LANES, (h + 1) * LANES)
        y = _rms(q_ref[0, :, cols].astype(F32), qg_ref[...])
        qo_ref[0, :, cols] = (_rope(y, c_ref, s1_ref, s2_ref, HEAD_DIM // 4) * scale).astype(BF16)
    for h in range(GQA_KV_HEADS):
        cols = slice(h * LANES, (h + 1) * LANES)
        y = _rms(k_ref[0, :, cols].astype(F32), kg_ref[...])
        ko_ref[0, :, cols] = _rope(y, c_ref, s1_ref, s2_ref, HEAD_DIM // 4).astype(BF16)


def gqa_prep(u, q_gain, k_gain, tables, q_col, k_col, ts):
    b, s, _ = u.shape
    wq = GQA_Q_HEADS * HEAD_DIM
    wk = GQA_KV_HEADS * HEAD_DIM
    tab = pl.BlockSpec((ts, LANES), lambda i, j: (j, 0))
    vec = pl.BlockSpec((1, LANES), lambda i, j: (0, 0))
    return pl.pallas_call(
        _gqa_prep_body,
        grid=(b, s // ts),
        in_specs=[pl.BlockSpec((1, ts, wq), lambda i, j: (i, j, q_col)),
                  pl.BlockSpec((1, ts, wk), lambda i, j: (i, j, k_col)),
                  vec, vec, tab, tab, tab],
        out_specs=[pl.BlockSpec((1, ts, wq), lambda i, j: (i, j, 0)),
                   pl.BlockSpec((1, ts, wk), lambda i, j: (i, j, 0))],
        out_shape=[jax.ShapeDtypeStruct((b, s, wq), BF16), jax.ShapeDtypeStruct((b, s, wk), BF16)],
        compiler_params=_params("parallel", "parallel"),
        name="gqa_prep",
    )(u, u, q_gain, k_gain, *tables)


def _mla_prep_body(cq_ref, ckv_ref, ckr_ref, cqg_ref, ckvg_ref, wq_ref, wk_ref, wr_ref, wv_ref,
                   qg_ref, kg_ref, c_ref, s1_ref, s2_ref, qo_ref, ko_ref, vo_ref):
    scale = MLA_QK_DIM ** -0.5
    cq = _rms(cq_ref[0].astype(F32), cqg_ref[...]).astype(BF16)
    ckv = _rms(ckv_ref[0].astype(F32), ckvg_ref[...]).astype(BF16)
    q = _dot(cq, wq_ref[...])
    k = _dot(ckv, wk_ref[...]) + _dot(ckr_ref[0], wr_ref[...])
    vo_ref[0] = _dot(ckv, wv_ref[...]).astype(BF16)

    def norm_rope(x, g_ref):
        ms = jnp.sum(x * x, axis=-1, keepdims=True) * (1.0 / MLA_QK_DIM)
        y = x * lax.rsqrt(ms + EPS) * g_ref[...]
        return _rope(y, c_ref, s1_ref, s2_ref, MLA_ROPE_DIM // 2)

    for h in range(MLA_HEADS):
        cols = slice(h * MLA_PAD_DIM, (h + 1) * MLA_PAD_DIM)
        qo_ref[0, :, cols] = (norm_rope(q[:, cols], qg_ref) * scale).astype(BF16)
        ko_ref[0, :, cols] = norm_rope(k[:, cols], kg_ref).astype(BF16)


def mla_prep(u, cq_gain, ckv_gain, w_q, w_k, w_r, w_v, q_gain, k_gain, tables, cq_col, ckv_col, ckr_col, ts):
    b, s, _ = u.shape
    q_rank, kv_rank = w_q.shape[0], w_k.shape[0]
    wqk = MLA_HEADS * MLA_PAD_DIM
    wv = MLA_HEADS * MLA_V_DIM
    tab = pl.BlockSpec((ts, MLA_PAD_DIM), lambda i, j: (j, 0))

    def full(a):
        return pl.BlockSpec(a.shape, lambda i, j: (0,) * a.ndim)

    return pl.pallas_call(
        _mla_prep_body,
        grid=(b, s // ts),
        in_specs=[pl.BlockSpec((1, ts, q_rank), lambda i, j: (i, j, cq_col)),
                  pl.BlockSpec((1, ts, kv_rank), lambda i, j: (i, j, ckv_col)),
                  pl.BlockSpec((1, ts, LANES), lambda i, j: (i, j, ckr_col)),
                  full(cq_gain), full(ckv_gain), full(w_q), full(w_k), full(w_r), full(w_v),
                  full(q_gain), full(k_gain), tab, tab, tab],
        out_specs=[pl.BlockSpec((1, ts, wqk), lambda i, j: (i, j, 0)),
                   pl.BlockSpec((1, ts, wqk), lambda i, j: (i, j, 0)),
                   pl.BlockSpec((1, ts, wv), lambda i, j: (i, j, 0))],
        out_shape=[jax.ShapeDtypeStruct((b, s, wqk), BF16), jax.ShapeDtypeStruct((b, s, wqk), BF16),
                   jax.ShapeDtypeStruct((b, s, wv), BF16)],
        compiler_params=_params("parallel", "parallel"),
        name="mla_prep",
    )(u, u, u, cq_gain, ckv_gain, w_q, w_k, w_r, w_v, q_gain, k_gain, *tables)


def _flash_body(q_ref, k_ref, v_ref, o_ref, m_scr, l_scr, acc_scr):
    j = pl.program_id(3)

    @pl.when(j == 0)
    def _():
        m_scr[...] = jnp.full(m_scr.shape, -jnp.inf, F32)
        l_scr[...] = jnp.zeros(l_scr.shape, F32)
        acc_scr[...] = jnp.zeros(acc_scr.shape, F32)

    s = lax.dot_general(q_ref[0], k_ref[0], (((1,), (1,)), ((), ())), preferred_element_type=F32)
    m_prev = m_scr[...]
    m_new = jnp.maximum(m_prev, jnp.max(s, axis=-1, keepdims=True))
    alpha = jnp.exp(m_prev - m_new)
    p = jnp.exp(s - m_new)
    l_scr[...] = alpha * l_scr[...] + jnp.sum(p, axis=-1, keepdims=True)
    acc_scr[...] = alpha * acc_scr[...] + _dot(p.astype(BF16), v_ref[0])
    m_scr[...] = m_new

    @pl.when(j == pl.num_programs(3) - 1)
    def _():
        o_ref[0] = (acc_scr[...] / l_scr[...]).astype(o_ref.dtype)


def flash_attention(q, k, v, n_maps, dk, dv, q_col, k_col, v_col, out_dtype, tq, tk):
    b, s, _ = q.shape
    return pl.pallas_call(
        _flash_body,
        grid=(b, n_maps, s // tq, s // tk),
        in_specs=[pl.BlockSpec((1, tq, dk), lambda i, h, a, c: (i, a, q_col(h))),
                  pl.BlockSpec((1, tk, dk), lambda i, h, a, c: (i, c, k_col(h))),
                  pl.BlockSpec((1, tk, dv), lambda i, h, a, c: (i, c, v_col(h)))],
        out_specs=pl.BlockSpec((1, tq, dv), lambda i, h, a, c: (i, a, h)),
        out_shape=jax.ShapeDtypeStruct((b, s, n_maps * dv), out_dtype),
        scratch_shapes=[pltpu.VMEM((tq, 1), F32), pltpu.VMEM((tq, 1), F32), pltpu.VMEM((tq, dv), F32)],
        compiler_params=_params("parallel", "parallel", "parallel", "arbitrary"),
        name="flash_attention",
    )(q, k, v)


def _diff_combine_body(o_ref, lam_ref, g_ref, y_ref, *, lambda_init):
    lam = lam_ref[...]
    lam_full = (jnp.exp(jnp.sum(lam[0:1] * lam[1:2], keepdims=True))
                - jnp.exp(jnp.sum(lam[2:3] * lam[3:4], keepdims=True)) + lambda_init)
    w = DIFF_HEADS * LANES
    for h in range(DIFF_HEADS):
        cols = slice(h * LANES, (h + 1) * LANES)
        d = o_ref[0, :, cols] - lam_full * o_ref[0, :, w + h * LANES:w + (h + 1) * LANES]
        y_ref[0, :, cols] = (_rms(d, g_ref[...]) * (1.0 - lambda_init)).astype(y_ref.dtype)


def diff_combine(o, lam, out_gain, lambda_init, ts):
    b, s, w2 = o.shape
    w = w2 // 2
    return pl.pallas_call(
        functools.partial(_diff_combine_body, lambda_init=lambda_init),
        grid=(b, s // ts),
        in_specs=[pl.BlockSpec((1, ts, w2), lambda i, j: (i, j, 0)),
                  pl.BlockSpec(lam.shape, lambda i, j: (0, 0)),
                  pl.BlockSpec((1, LANES), lambda i, j: (0, 0))],
        out_specs=pl.BlockSpec((1, ts, w), lambda i, j: (i, j, 0)),
        out_shape=jax.ShapeDtypeStruct((b, s, w), BF16),
        compiler_params=_params("parallel", "parallel"),
        name="diff_combine",
    )(o, lam, out_gain)


def _merge_body(xn_ref, ya_ref, yb_ref, yc_ref, yd_ref, wg0, wg1, wg2, wg3, p0, p1, p2, p3, o_ref):
    xn = xn_ref[...]
    acc = None
    for y_ref, wg_ref, p_ref in ((ya_ref, wg0, p0), (yb_ref, wg1, p1), (yc_ref, wg2, p2), (yd_ref, wg3, p3)):
        term = _sigmoid(_dot(xn, wg_ref[...])) * _dot(y_ref[...], p_ref[0])
        acc = term if acc is None else acc + term
    o_ref[...] = acc.astype(o_ref.dtype)


def gated_merge(xn, ys, w_gate, w_branch, tm, tn):
    m, d = xn.shape
    width = w_branch.shape[1]
    nj = d // tn
    y_spec = pl.BlockSpec((tm, width), lambda i, j: (i, 0))
    wg_specs = [pl.BlockSpec((d, tn), functools.partial(lambda i, j, br: (0, br * nj + j), br=br)) for br in range(4)]
    p_specs = [pl.BlockSpec((1, width, tn), functools.partial(lambda i, j, br: (br, 0, j), br=br)) for br in range(4)]
    return pl.pallas_call(
        _merge_body,
        grid=(m // tm, nj),
        in_specs=[pl.BlockSpec((tm, d), lambda i, j: (i, 0)), y_spec, y_spec, y_spec, y_spec] + wg_specs + p_specs,
        out_specs=pl.BlockSpec((tm, tn), lambda i, j: (i, j)),
        out_shape=jax.ShapeDtypeStruct((m, d), BF16),
        compiler_params=_params("parallel", "arbitrary"),
        name="gated_merge",
    )(xn, *ys, w_gate, w_gate, w_gate, w_gate, w_branch, w_branch, w_branch, w_branch)


def _router_body(x_ref, g_ref, r_ref, idx_ref, w_ref, *, n_experts):
    xn = _rms(x_ref[...], g_ref[...])
    logits = jnp.dot(xn, r_ref[...], precision=lax.Precision.HIGHEST, preferred_element_type=F32)
    lane = lax.broadcasted_iota(jnp.int32, logits.shape, 1).astype(F32)
    neg = -jnp.inf
    lg = jnp.where(lane < n_experts, logits, neg)
    v1 = jnp.max(lg, axis=-1, keepdims=True)
    i1 = jnp.min(jnp.where(lg == v1, lane, float(LANES)), axis=-1, keepdims=True)
    lg2 = jnp.where(lane == i1, neg, lg)
    v2 = jnp.max(lg2, axis=-1, keepdims=True)
    i2 = jnp.min(jnp.where(lg2 == v2, lane, float(LANES)), axis=-1, keepdims=True)
    e21 = jnp.exp(v2 - v1)
    w1 = 1.0 / (1.0 + e21)
    w2 = e21 * w1
    idx_ref[...] = jnp.where(lane == 0.0, i1, jnp.where(lane == 1.0, i2, 0.0)).astype(jnp.int32)
    w_ref[...] = jnp.where(lane == 0.0, w1, jnp.where(lane == 1.0, w2, 0.0))


def moe_router(x, g, router_padded, n_experts, tm):
    m, d = x.shape
    return pl.pallas_call(
        functools.partial(_router_body, n_experts=n_experts),
        grid=(m // tm,),
        in_specs=[pl.BlockSpec((tm, d), lambda i: (i, 0)),
                  pl.BlockSpec((1, d), lambda i: (0, 0)),
                  pl.BlockSpec((d, LANES), lambda i: (0, 0))],
        out_specs=[pl.BlockSpec((tm, LANES), lambda i: (i, 0)), pl.BlockSpec((tm, LANES), lambda i: (i, 0))],
        out_shape=[jax.ShapeDtypeStruct((m, LANES), jnp.int32), jax.ShapeDtypeStruct((m, LANES), F32)],
        compiler_params=_params("parallel"),
        name="moe_router",
    )(x, g, router_padded)


def _moe_up_body(te_ref, nv_ref, tok_ref, x_hbm, g_ref, wg_ref, wu_ref, o_ref, xbuf, xn_scr, sem):
    t = pl.program_id(0)
    f = pl.program_id(1)
    tm = xbuf.shape[0]

    def row_copy(r, tok):
        return pltpu.make_async_copy(x_hbm.at[pl.ds(tok, 1)], xbuf.at[pl.ds(r, 1)], sem)

    @pl.when(t < nv_ref[0])
    def _():
        @pl.when(f == 0)
        def _():
            def start(r, c):
                row_copy(r, tok_ref[t * tm + r]).start()
                return c

            def wait(r, c):
                row_copy(r, 0).wait()
                return c

            lax.fori_loop(0, tm, start, 0)
            lax.fori_loop(0, tm, wait, 0)
            xn_scr[...] = _rms(xbuf[...], g_ref[...]).astype(BF16)

        xn = xn_scr[...]
        gate = _dot(xn, wg_ref[0])
        up = _dot(xn, wu_ref[0])
        o_ref[...] = (gate * _sigmoid(gate) * up).astype(o_ref.dtype)

    @pl.when(t >= nv_ref[0])
    def _():
        o_ref[...] = jnp.zeros(o_ref.shape, o_ref.dtype)


def moe_up(tile_expert, n_valid, row_token, x, g, w_gate_up, tm, tf):
    d = x.shape[1]
    f = w_gate_up.shape[2] // 2
    nf = f // tf
    p = row_token.shape[0]

    def wmap(off, t, j, te, nv, tok):
        return (te[t], 0, off + jnp.where(t < nv[0], j, nf - 1))

    grid_spec = pltpu.PrefetchScalarGridSpec(
        num_scalar_prefetch=3,
        grid=(p // tm, nf),
        in_specs=[pl.BlockSpec(memory_space=pl.ANY),
                  pl.BlockSpec((1, d), lambda t, j, te, nv, tok: (0, 0)),
                  pl.BlockSpec((1, d, tf), functools.partial(wmap, 0)),
                  pl.BlockSpec((1, d, tf), functools.partial(wmap, nf))],
        out_specs=pl.BlockSpec((tm, tf), lambda t, j, te, nv, tok: (t, j)),
        scratch_shapes=[pltpu.VMEM((tm, d), F32), pltpu.VMEM((tm, d), BF16), pltpu.SemaphoreType.DMA(())],
    )
    return pl.pallas_call(
        _moe_up_body,
        grid_spec=grid_spec,
        out_shape=jax.ShapeDtypeStruct((p, f), BF16),
        compiler_params=_params("arbitrary", "arbitrary"),
        name="moe_up",
    )(tile_expert, n_valid, row_token, x, g, w_gate_up, w_gate_up)


def _moe_down_body(te_ref, nv_ref, h_ref, w_ref, rw_ref, o_ref):
    t = pl.program_id(0)

    @pl.when(t < nv_ref[0])
    def _():
        o_ref[...] = rw_ref[...] * _dot(h_ref[...], w_ref[0])

    @pl.when(t >= nv_ref[0])
    def _():
        o_ref[...] = jnp.zeros(o_ref.shape, o_ref.dtype)


def moe_down(tile_expert, n_valid, h, w_down, row_w, tm, tn):
    p, f = h.shape
    d = w_down.shape[2]
    nj = d // tn
    grid_spec = pltpu.PrefetchScalarGridSpec(
        num_scalar_prefetch=2,
        grid=(p // tm, nj),
        in_specs=[pl.BlockSpec((tm, f), lambda t, j, te, nv: (t, 0)),
                  pl.BlockSpec((1, f, tn), lambda t, j, te, nv: (te[t], 0, jnp.where(t < nv[0], j, nj - 1))),
                  pl.BlockSpec((tm, 1), lambda t, j, te, nv: (t, 0))],
        out_specs=pl.BlockSpec((tm, tn), lambda t, j, te, nv: (t, j)),
    )
    return pl.pallas_call(
        _moe_down_body,
        grid_spec=grid_spec,
        out_shape=jax.ShapeDtypeStruct((p, d), F32),
        compiler_params=_params("arbitrary", "arbitrary"),
        name="moe_down",
    )(tile_expert, n_valid, h, w_down, row_w)


def _moe_combine_body(pos_ref, x_ref, y_hbm, o_ref, buf0, buf1, sem):
    i = pl.program_id(0)
    tc = buf0.shape[0]

    def row_copy(r, k, src):
        dst = buf0 if k == 0 else buf1
        return pltpu.make_async_copy(y_hbm.at[pl.ds(src, 1)], dst.at[pl.ds(r, 1)], sem)

    def start(r, c):
        for k in range(TOP_K):
            row_copy(r, k, pos_ref[(i * tc + r) * TOP_K + k]).start()
        return c

    def wait(r, c):
        for k in range(TOP_K):
            row_copy(r, k, 0).wait()
        return c

    lax.fori_loop(0, tc, start, 0)
    lax.fori_loop(0, tc, wait, 0)
    o_ref[...] = x_ref[...] + buf0[...] + buf1[...]


def moe_combine(pos, x, y, tc):
    m, d = x.shape
    grid_spec = pltpu.PrefetchScalarGridSpec(
        num_scalar_prefetch=1,
        grid=(m // tc,),
        in_specs=[pl.BlockSpec((tc, d), lambda i, pos: (i, 0)),
                  pl.BlockSpec(memory_space=pl.ANY)],
        out_specs=pl.BlockSpec((tc, d), lambda i, pos: (i, 0)),
        scratch_shapes=[pltpu.VMEM((tc, d), F32), pltpu.VMEM((tc, d), F32), pltpu.SemaphoreType.DMA(())],
    )
    return pl.pallas_call(
        _moe_combine_body,
        grid_spec=grid_spec,
        out_shape=jax.ShapeDtypeStruct((m, d), F32),
        compiler_params=_params("arbitrary"),
        name="moe_combine",
    )(pos, x, y)


def _moe_schedule(idx, wts, n_experts, tm):
    m = idx.shape[0]
    n_assign = m * TOP_K
    e_flat = idx.reshape(n_assign)
    onehot = (e_flat[:, None] == jnp.arange(n_experts, dtype=jnp.int32)[None, :]).astype(jnp.int32)
    csum = jnp.cumsum(onehot, axis=0)
    rank = jnp.take_along_axis(csum, e_flat[:, None], axis=1)[:, 0] - 1
    counts = csum[-1]
    padded = ((counts + tm - 1) // tm) * tm
    ends = jnp.cumsum(padded)
    pos = (ends - padded)[e_flat] + rank
    n_rows = n_assign + n_experts * tm
    row_token = jnp.zeros((n_rows,), jnp.int32).at[pos].set(jnp.arange(n_assign, dtype=jnp.int32) // TOP_K)
    row_w = jnp.zeros((n_rows,), F32).at[pos].set(wts.reshape(n_assign))
    n_tiles = n_rows // tm
    tiles = jnp.arange(n_tiles, dtype=jnp.int32)
    n_valid = (ends[-1] // tm).astype(jnp.int32)
    tile_expert = jnp.minimum(jnp.searchsorted(ends, tiles * tm, side="right"), n_experts - 1).astype(jnp.int32)
    tile_expert = jnp.where(tiles < n_valid, tile_expert, tile_expert[n_valid - 1])
    return pos.astype(jnp.int32), row_token, row_w.reshape(n_rows, 1), tile_expert, n_valid.reshape(1)


def moe_ffn(x, g, router, w_gate_up, w_down):
    m, d = x.shape
    n_experts = router.shape[1]
    tm = _tile(m, 512)
    router_padded = jnp.pad(router, ((0, 0), (0, LANES - n_experts)))
    idx, wts = moe_router(x, g, router_padded, n_experts, _tile(m, 512))
    pos, row_token, row_w, tile_expert, n_valid = _moe_schedule(idx[:, :TOP_K], wts[:, :TOP_K], n_experts, tm)
    f = w_down.shape[1]
    h = moe_up(tile_expert, n_valid, row_token, x, g, w_gate_up, tm, _tile(f, 512))
    y = moe_down(tile_expert, n_valid, h, w_down, row_w, tm, _tile(d, 512))
    return moe_combine(pos, x, y, _tile(m, 256))


def _rope_cos_sin(pos, dim, theta):
    inv = jnp.power(jnp.float32(theta), -jnp.arange(0, dim, 2, dtype=F32) / dim)
    ang = pos.astype(F32)[:, None] * inv[None, :]
    return jnp.cos(ang), jnp.sin(ang)


def _rope_tables(s):
    pos = jnp.arange(s, dtype=jnp.int32)
    rows = s // GRID_W
    row_pos = jnp.broadcast_to(jnp.arange(rows, dtype=jnp.int32)[:, None], (rows, GRID_W)).reshape(-1)
    col_pos = jnp.broadcast_to(jnp.arange(GRID_W, dtype=jnp.int32)[None, :], (rows, GRID_W)).reshape(-1)

    def ones(n):
        return jnp.ones((s, n), F32)

    def zeros(n):
        return jnp.zeros((s, n), F32)

    cat = functools.partial(jnp.concatenate, axis=1)
    cos, sin = _rope_cos_sin(pos, DIFF_ROPE_DIM, ROPE_THETA)
    rest = DIFF_QK_DIM - DIFF_ROPE_DIM
    half = DIFF_ROPE_DIM // 2
    diff = (cat([cos, cos, ones(rest)] * 2), cat([-sin, zeros(half), zeros(rest)] * 2),
            cat([zeros(half), sin, zeros(rest)] * 2))
    cos, sin = _rope_cos_sin(pos, MLA_ROPE_DIM, ROPE_THETA)
    half = MLA_ROPE_DIM // 2
    tail = MLA_PAD_DIM - MLA_QK_DIM
    mla = (cat([ones(MLA_NOPE_DIM), cos, cos, ones(tail)]),
           cat([zeros(MLA_NOPE_DIM), -sin, zeros(half), zeros(tail)]),
           cat([zeros(MLA_NOPE_DIM), zeros(half), sin, zeros(tail)]))
    cos_r, sin_r = _rope_cos_sin(row_pos, HEAD_DIM // 2, AXIAL_THETA)
    cos_c, sin_c = _rope_cos_sin(col_pos, HEAD_DIM // 2, AXIAL_THETA)
    half = HEAD_DIM // 4
    gqa = (cat([cos_r, cos_r, cos_c, cos_c]), cat([-sin_r, zeros(half), -sin_c, zeros(half)]),
           cat([zeros(half), sin_r, zeros(half), sin_c]))
    return diff, mla, gqa


COL_A_GATE, COL_A_X, COL_B_Q, COL_B_K, COL_B_V, COL_C_Q, COL_D_Q = (i * 512 for i in range(7))
COL_C_KV, COL_D_K, COL_D_V, COL_C_KR, MIX_COLS = 3584, 3840, 4096, 4352, 4480


def _mixer_in_weights(w_in_l):
    widths = (512, 512, 512, 512, 512, 512, 256, 64, 512, 256, 256)
    offs = [0]
    for wd in widths:
        offs.append(offs[-1] + wd)
    (a_gate, a_x, b_q, b_k, b_v, c_q, c_kv, c_kr, d_q, d_k, d_v) = (
        w_in_l[:, offs[i]:offs[i + 1]] for i in range(len(widths)))
    pad = jnp.zeros((w_in_l.shape[0], LANES - MLA_ROPE_DIM), w_in_l.dtype)
    w_mix = jnp.concatenate([a_gate, a_x, b_q, b_k, b_v, c_q, d_q, c_kv, d_k, d_v, c_kr, pad], axis=1)
    return w_mix.astype(BF16), w_in_l[:, offs[-1]:].astype(BF16)


def _mla_weights(w_uq, w_ukv):
    q_rank, kv_rank = w_uq.shape[0], w_ukv.shape[0]
    wq = w_uq.reshape(q_rank, MLA_HEADS, MLA_QK_DIM)
    wq = jnp.pad(wq, ((0, 0), (0, 0), (0, MLA_PAD_DIM - MLA_QK_DIM))).reshape(q_rank, MLA_HEADS * MLA_PAD_DIM)
    wkv = w_ukv.reshape(kv_rank, MLA_HEADS, MLA_NOPE_DIM + MLA_V_DIM)
    wk = jnp.pad(wkv[:, :, :MLA_NOPE_DIM], ((0, 0), (0, 0), (0, MLA_PAD_DIM - MLA_NOPE_DIM)))
    wk = wk.reshape(kv_rank, MLA_HEADS * MLA_PAD_DIM)
    wv = wkv[:, :, MLA_NOPE_DIM:].reshape(kv_rank, MLA_HEADS * MLA_V_DIM)
    place = jnp.zeros((LANES, MLA_HEADS, MLA_PAD_DIM), F32)
    r = jnp.arange(MLA_ROPE_DIM)
    place = place.at[r, :, MLA_NOPE_DIM + r].set(1.0).reshape(LANES, MLA_HEADS * MLA_PAD_DIM)
    return wq.astype(BF16), wk.astype(BF16), place.astype(BF16), wv.astype(BF16)


def _lru_gate_weights(gate_w):
    nb = gate_w.shape[2]
    per = LANES // LRU_BLOCK_DIM
    w = gate_w.reshape(4, nb // per, per, LRU_BLOCK_DIM, LRU_BLOCK_DIM)
    bd = jnp.einsum("dcpkj,pq->cdpkqj", w, jnp.eye(per, dtype=w.dtype))
    return bd.reshape(nb // per, 4, LANES, LANES).astype(BF16)


def _pad_gain(g, width):
    return jnp.pad(g, (0, width - g.shape[0])).reshape(1, width)


def kernel(x, norm_mix, w_in, lru_conv_w, lru_conv_b, lru_gate_w, lru_gate_b, lru_lambda, diff_q_norm, diff_k_norm, diff_lambda, diff_out_norm, mla_cq_norm, mla_ckv_norm, mla_w_uq, mla_w_ukv, mla_q_norm, mla_k_norm, gqa_q_norm, gqa_k_norm, w_branch, w_out, norm_ffn, ffn_w_gate_up, ffn_w_down, moe_router, moe_w_gate_up, moe_w_down):
    b, s, d = x.shape
    m = b * s
    depth = w_in.shape[0]
    lru_width = lru_conv_w.shape[2]
    tables_diff, tables_mla, tables_gqa = _rope_tables(s)
    ts = _tile(s, 512)
    tq, tk = _tile(s, 512), _tile(s, 512)
    tm = _tile(m, 512)
    xf = x.reshape(m, d)

    for l in range(depth):
        w_mix, w_gate = _mixer_in_weights(w_in[l])
        u, xn = norm_proj(xf, norm_mix[l].reshape(1, d), w_mix, tm, _tile(MIX_COLS, 640))
        u = u.reshape(b, s, MIX_COLS)

        y_a = lru_mixer(u, lru_conv_w[l], lru_conv_b[l].reshape(1, lru_width), _lru_gate_weights(lru_gate_w[l]),
                        lru_gate_b[l].reshape(4, lru_width), lru_lambda[l], COL_A_GATE // LANES, COL_A_X // LANES)

        lambda_init = 0.8 - 0.6 * math.exp(-0.3 * l)
        q_b, k_b = diff_prep(u, jnp.tile(diff_q_norm[l], 2).reshape(1, LANES), jnp.tile(diff_k_norm[l], 2).reshape(1, LANES),
                             tables_diff, COL_B_Q // 512, COL_B_K // 512, ts)
        o_b = flash_attention(q_b, k_b, u, 2 * DIFF_HEADS, LANES, LANES,
                              lambda h: h, lambda h: h % DIFF_HEADS, lambda h: COL_B_V // LANES + h % DIFF_HEADS,
                              F32, tq, tk)
        y_b = diff_combine(o_b, diff_lambda[l], diff_out_norm[l].reshape(1, LANES), lambda_init, ts)

        wq, wk, wr, wv = _mla_weights(mla_w_uq[l], mla_w_ukv[l])
        q_c, k_c, v_c = mla_prep(u, mla_cq_norm[l].reshape(1, -1), mla_ckv_norm[l].reshape(1, -1), wq, wk, wr, wv,
                                 _pad_gain(mla_q_norm[l], MLA_PAD_DIM), _pad_gain(mla_k_norm[l], MLA_PAD_DIM),
                                 tables_mla, COL_C_Q // 512, COL_C_KV // 256, COL_C_KR // LANES, ts)
        y_c = flash_attention(q_c, k_c, v_c, MLA_HEADS, MLA_PAD_DIM, MLA_V_DIM,
                              lambda h: h, lambda h: h, lambda h: h, BF16, tq, tk)

        group = GQA_Q_HEADS // GQA_KV_HEADS
        q_d, k_d = gqa_prep(u, gqa_q_norm[l].reshape(1, LANES), gqa_k_norm[l].reshape(1, LANES), tables_gqa,
                            COL_D_Q // 512, COL_D_K // 256, ts)
        y_d = flash_attention(q_d, k_d, u, GQA_Q_HEADS, HEAD_DIM, HEAD_DIM,
                              lambda h: h, lambda h: h // group, lambda h: COL_D_V // LANES + h // group,
                              BF16, tq, tk)

        ys = [y.reshape(m, -1) for y in (y_a, y_b, y_c, y_d)]
        merged = gated_merge(xn, ys, w_gate, w_branch[l].astype(BF16), tm, _tile(d, 512))
        xf = mm_residual(merged, w_out[l].astype(BF16), xf, tm, _tile(d, 512))

        g_ffn = norm_ffn[l].reshape(1, d)
        if l % 2 == 0:
            f = ffn_w_down.shape[1]
            act = norm_swiglu(xf, g_ffn, ffn_w_gate_up[l // 2].astype(BF16), tm, _tile(f, 512))
            xf = mm_residual(act, ffn_w_down[l // 2].astype(BF16), xf, tm, _tile(d, 512))
        else:
            xf = moe_ffn(xf, g_ffn, moe_router[l // 2], moe_w_gate_up[l // 2].astype(BF16),
                         moe_w_down[l // 2].astype(BF16))
    return xf.reshape(b, s, d)
```

```python
import functools
import math

import jax
import jax.numpy as jnp
from jax import lax
from jax.experimental import pallas as pl
from jax.experimental.pallas import tpu as pltpu

F32 = jnp.float32
BF16 = jnp.bfloat16
EPS = 1e-6
LANES = 128

LRU_BLOCK_DIM = 64
LRU_C = 8.0
DIFF_HEADS = 4
DIFF_QK_DIM = 64
DIFF_ROPE_DIM = 16
MLA_HEADS = 4
MLA_NOPE_DIM = 128
MLA_ROPE_DIM = 64
MLA_QK_DIM = MLA_NOPE_DIM + MLA_ROPE_DIM
MLA_PAD_DIM = 256
MLA_V_DIM = 128
GQA_Q_HEADS = 4
GQA_KV_HEADS = 2
HEAD_DIM = 128
GRID_W = 64
ROPE_THETA = 500000.0
AXIAL_THETA = 10000.0
TOP_K = 2
LOG2E = 1.4426950408889634
VT_PAD = 16

VMEM_LIMIT = 56 * 1024 * 1024


def _params(*sem):
    return pltpu.CompilerParams(dimension_semantics=sem, vmem_limit_bytes=VMEM_LIMIT)


def _tile(dim, pref):
    return pref if dim % pref == 0 else dim


def _sigmoid(x):
    return 1.0 / (1.0 + jnp.exp(-x))


def _dot(a, b):
    return jnp.dot(a, b, preferred_element_type=F32)


def _rms(x, g):
    ms = jnp.mean(x * x, axis=-1, keepdims=True)
    return x * lax.rsqrt(ms + EPS) * g


def _norm_proj_body(x_ref, g_ref, w_ref, o_ref, xn_ref, xn_scr):
    @pl.when(pl.program_id(1) == 0)
    def _():
        xn = _rms(x_ref[...], g_ref[...]).astype(BF16)
        xn_scr[...] = xn
        xn_ref[...] = xn

    o_ref[...] = _dot(xn_scr[...], w_ref[...]).astype(o_ref.dtype)


def norm_proj(x, g, w, tm, tn):
    m, d = x.shape
    n = w.shape[1]
    return pl.pallas_call(
        _norm_proj_body,
        grid=(m // tm, n // tn),
        in_specs=[pl.BlockSpec((tm, d), lambda i, j: (i, 0)),
                  pl.BlockSpec((1, d), lambda i, j: (0, 0)),
                  pl.BlockSpec((d, tn), lambda i, j: (0, j))],
        out_specs=[pl.BlockSpec((tm, tn), lambda i, j: (i, j)),
                   pl.BlockSpec((tm, d), lambda i, j: (i, 0))],
        out_shape=[jax.ShapeDtypeStruct((m, n), BF16), jax.ShapeDtypeStruct((m, d), BF16)],
        scratch_shapes=[pltpu.VMEM((tm, d), BF16)],
        compiler_params=_params("parallel", "arbitrary"),
        name="norm_proj",
    )(x, g, w)


def _norm_swiglu_body(x_ref, g_ref, wg_ref, wu_ref, o_ref, xn_scr):
    @pl.when(pl.program_id(1) == 0)
    def _():
        xn_scr[...] = _rms(x_ref[...], g_ref[...]).astype(BF16)

    xn = xn_scr[...]
    gate = _dot(xn, wg_ref[...])
    up = _dot(xn, wu_ref[...])
    o_ref[...] = (gate * _sigmoid(gate) * up).astype(o_ref.dtype)


def norm_swiglu(x, g, w_gate_up, tm, tf):
    m, d = x.shape
    f = w_gate_up.shape[1] // 2
    nf = f // tf
    return pl.pallas_call(
        _norm_swiglu_body,
        grid=(m // tm, nf),
        in_specs=[pl.BlockSpec((tm, d), lambda i, j: (i, 0)),
                  pl.BlockSpec((1, d), lambda i, j: (0, 0)),
                  pl.BlockSpec((d, tf), lambda i, j: (0, j)),
                  pl.BlockSpec((d, tf), lambda i, j: (0, nf + j))],
        out_specs=pl.BlockSpec((tm, tf), lambda i, j: (i, j)),
        out_shape=jax.ShapeDtypeStruct((m, f), BF16),
        scratch_shapes=[pltpu.VMEM((tm, d), BF16)],
        compiler_params=_params("parallel", "arbitrary"),
        name="norm_swiglu",
    )(x, g, w_gate_up, w_gate_up)


def _mm_residual_body(a_ref, b_ref, r_ref, o_ref):
    o_ref[...] = r_ref[...] + _dot(a_ref[...], b_ref[...])


def mm_residual(a, b, res, tm, tn):
    m, k = a.shape
    n = b.shape[1]
    return pl.pallas_call(
        _mm_residual_body,
        grid=(m // tm, n // tn),
        in_specs=[pl.BlockSpec((tm, k), lambda i, j: (i, 0)),
                  pl.BlockSpec((k, tn), lambda i, j: (0, j)),
                  pl.BlockSpec((tm, tn), lambda i, j: (i, j))],
        out_specs=pl.BlockSpec((tm, tn), lambda i, j: (i, j)),
        out_shape=jax.ShapeDtypeStruct((m, n), F32),
        compiler_params=_params("parallel", "arbitrary"),
        name="mm_residual",
    )(a, b, res)


def _lru_body(ug_ref, ux_ref, cw_ref, cb_ref, gw_ref, gb_ref, lam_ref, o_ref, xpad, hf, *, chunk):
    s = ux_ref.shape[1]
    t = chunk
    nc = s // t
    halo = 8
    xpad[0:halo, :] = jnp.zeros((halo, LANES), F32)
    xpad[s + halo:s + 2 * halo, :] = jnp.zeros((halo, LANES), F32)
    xpad[halo:s + halo, :] = ux_ref[0].astype(F32)

    z = -lam_ref[...]
    softplus = jnp.maximum(z, 0.0) + jnp.log(1.0 + jnp.exp(-jnp.abs(z)))
    decay = -LRU_C * softplus
    row = lax.broadcasted_iota(jnp.int32, (t, LANES), 0)

    def coeffs(c, d):
        ext = xpad[pl.ds(pl.multiple_of(c * t, 8), t + 2 * halo), :]
        xc = cb_ref[...] + sum(cw_ref[j:j + 1, :] * ext[halo - 2 + j:halo - 2 + j + t] for j in range(4))
        xcb = xc.astype(BF16)
        r = _sigmoid(_dot(xcb, gw_ref[0, 2 * d]) + gb_ref[2 * d:2 * d + 1, :])
        i = _sigmoid(_dot(xcb, gw_ref[0, 2 * d + 1]) + gb_ref[2 * d + 1:2 * d + 2, :])
        log_a = decay[d:d + 1, :] * r
        a = jnp.exp(log_a)
        b = jnp.sqrt(1.0 - jnp.exp(2.0 * log_a)) * (i * xc)
        return a, b

    def scan(a, b, reverse):
        k = 1
        while k < t:
            if reverse:
                keep, shift = row < t - k, t - k
            else:
                keep, shift = row >= k, k
            a_s = jnp.where(keep, pltpu.roll(a, shift, 0), 1.0)
            b_s = jnp.where(keep, pltpu.roll(b, shift, 0), 0.0)
            b = a * b_s + b
            a = a * a_s
            k *= 2
        return a, b

    def fwd(c, carry):
        a, b = scan(*coeffs(c, 0), reverse=False)
        h = b + a * carry
        hf[pl.ds(pl.multiple_of(c * t, 8), t), :] = h
        return h[t - 1:t, :]

    lax.fori_loop(0, nc, fwd, jnp.zeros((1, LANES), F32))

    def bwd(i, carry):
        c = nc - 1 - i
        a, b = scan(*coeffs(c, 1), reverse=True)
        h = b + a * carry
        rows = pl.ds(pl.multiple_of(c * t, 16), t)
        g = ug_ref[0, rows, :].astype(F32)
        gelu = 0.5 * g * (1.0 + jnp.tanh(0.7978845608028654 * (g + 0.044715 * (g * g * g))))
        o_ref[0, rows, :] = ((hf[rows, :] + h) * gelu).astype(o_ref.dtype)
        return h[0:1, :]

    lax.fori_loop(0, nc, bwd, jnp.zeros((1, LANES), F32))


def lru_mixer(u, conv_w, conv_b, gate_w_bd, gate_b, lam, gate_col, x_col):
    b, s, _ = u.shape
    w = conv_w.shape[1]
    chunk = _tile(s, 64)
    return pl.pallas_call(
        functools.partial(_lru_body, chunk=chunk),
        grid=(b, w // LANES),
        in_specs=[pl.BlockSpec((1, s, LANES), lambda i, c: (i, 0, gate_col + c)),
                  pl.BlockSpec((1, s, LANES), lambda i, c: (i, 0, x_col + c)),
                  pl.BlockSpec((4, LANES), lambda i, c: (0, c)),
                  pl.BlockSpec((1, LANES), lambda i, c: (0, c)),
                  pl.BlockSpec((1, 4, LANES, LANES), lambda i, c: (c, 0, 0, 0)),
                  pl.BlockSpec((4, LANES), lambda i, c: (0, c)),
                  pl.BlockSpec((2, LANES), lambda i, c: (0, c))],
        out_specs=pl.BlockSpec((1, s, LANES), lambda i, c: (i, 0, c)),
        out_shape=jax.ShapeDtypeStruct((b, s, w), BF16),
        scratch_shapes=[pltpu.VMEM((s + 16, LANES), F32), pltpu.VMEM((s, LANES), F32)],
        compiler_params=_params("parallel", "parallel"),
        name="lru_mixer",
    )(u, u, conv_w, conv_b, gate_w_bd, gate_b, lam)


def _rope(y, c_ref, s1_ref, s2_ref, shift):
    w = y.shape[-1]
    return y * c_ref[...] + pltpu.roll(y, w - shift, 1) * s1_ref[...] + pltpu.roll(y, shift, 1) * s2_ref[...]


def _store_vt(vt_ref, h, v):
    ts, dv = v.shape
    vt_ref[0, h, 0, 0:dv, :] = v.T.astype(BF16)
    vt_ref[0, h, 0, dv:dv + VT_PAD, :] = jnp.ones((VT_PAD, ts), BF16)


def _vt_spec_shape(b, heads, s, ts, dv):
    spec = pl.BlockSpec((1, heads, 1, dv + VT_PAD, ts), lambda i, j: (i, 0, j, 0, 0))
    return spec, jax.ShapeDtypeStruct((b, heads, s // ts, dv + VT_PAD, ts), BF16)


def _diff_prep_body(q_ref, k_ref, v_ref, qg_ref, kg_ref, c_ref, s1_ref, s2_ref, qo_ref, ko_ref, vt_ref):
    lane = lax.broadcasted_iota(jnp.int32, (1, LANES), 1)
    low = lane < DIFF_QK_DIM
    scale = DIFF_QK_DIM ** -0.5 * LOG2E
    nh = DIFF_HEADS

    def norm_rope(x, g_ref):
        x2 = x * x
        ss_lo = jnp.sum(jnp.where(low, x2, 0.0), axis=-1, keepdims=True)
        ss_hi = jnp.sum(jnp.where(low, 0.0, x2), axis=-1, keepdims=True)
        ms = jnp.where(low, ss_lo, ss_hi) * (1.0 / DIFF_QK_DIM)
        y = x * lax.rsqrt(ms + EPS) * g_ref[...]
        return _rope(y, c_ref, s1_ref, s2_ref, DIFF_ROPE_DIM // 2)

    for h in range(nh):
        cols = slice(h * LANES, (h + 1) * LANES)
        q = norm_rope(q_ref[0, :, cols].astype(F32), qg_ref) * scale
        qo_ref[0, :, cols] = jnp.where(low, q, 0.0).astype(BF16)
        qo_ref[0, :, nh * LANES + h * LANES:nh * LANES + (h + 1) * LANES] = jnp.where(low, 0.0, q).astype(BF16)
        ko_ref[0, :, cols] = norm_rope(k_ref[0, :, cols].astype(F32), kg_ref).astype(BF16)
        _store_vt(vt_ref, h, v_ref[0, :, cols].astype(F32))


def diff_prep(u, q_gain, k_gain, tables, q_col, k_col, v_col, ts):
    b, s, _ = u.shape
    w = DIFF_HEADS * LANES
    tab = pl.BlockSpec((ts, LANES), lambda i, j: (j, 0))
    vec = pl.BlockSpec((1, LANES), lambda i, j: (0, 0))
    vt_spec, vt_shape = _vt_spec_shape(b, DIFF_HEADS, s, ts, LANES)
    return pl.pallas_call(
        _diff_prep_body,
        grid=(b, s // ts),
        in_specs=[pl.BlockSpec((1, ts, w), lambda i, j: (i, j, q_col)),
                  pl.BlockSpec((1, ts, w), lambda i, j: (i, j, k_col)),
                  pl.BlockSpec((1, ts, w), lambda i, j: (i, j, v_col)),
                  vec, vec, tab, tab, tab],
        out_specs=[pl.BlockSpec((1, ts, 2 * w), lambda i, j: (i, j, 0)),
                   pl.BlockSpec((1, ts, w), lambda i, j: (i, j, 0)), vt_spec],
        out_shape=[jax.ShapeDtypeStruct((b, s, 2 * w), BF16), jax.ShapeDtypeStruct((b, s, w), BF16), vt_shape],
        compiler_params=_params("parallel", "parallel"),
        name="diff_prep",
    )(u, u, u, q_gain, k_gain, *tables)


def _gqa_prep_body(q_ref, k_ref, v_ref, qg_ref, kg_ref, c_ref, s1_ref, s2_ref, qo_ref, ko_ref, vt_ref):
    scale = HEAD_DIM ** -0.5 * LOG2E
    for h in range(GQA_Q_HEADS):
        cols = slice(h * LANES, (h + 1) * LANES)
        y = _rms(q_ref[0, :, cols].astype(F32), qg_ref[...])
        qo_ref[0, :, cols] = (_rope(y, c_ref, s1_ref, s2_ref, HEAD_DIM // 4) * scale).astype(BF16)
    for h in range(GQA_KV_HEADS):
        cols = slice(h * LANES, (h + 1) * LANES)
        y = _rms(k_ref[0, :, cols].astype(F32), kg_ref[...])
        ko_ref[0, :, cols] = _rope(y, c_ref, s1_ref, s2_ref, HEAD_DIM // 4).astype(BF16)
        _store_vt(vt_ref, h, v_ref[0, :, cols].astype(F32))


def gqa_prep(u, q_gain, k_gain, tables, q_col, k_col, v_col, ts):
    b, s, _ = u.shape
    wq = GQA_Q_HEADS * HEAD_DIM
    wk = GQA_KV_HEADS * HEAD_DIM
    tab = pl.BlockSpec((ts, LANES), lambda i, j: (j, 0))
    vec = pl.BlockSpec((1, LANES), lambda i, j: (0, 0))
    vt_spec, vt_shape = _vt_spec_shape(b, GQA_KV_HEADS, s, ts, HEAD_DIM)
    return pl.pallas_call(
        _gqa_prep_body,
        grid=(b, s // ts),
        in_specs=[pl.BlockSpec((1, ts, wq), lambda i, j: (i, j, q_col)),
                  pl.BlockSpec((1, ts, wk), lambda i, j: (i, j, k_col)),
                  pl.BlockSpec((1, ts, wk), lambda i, j: (i, j, v_col)),
                  vec, vec, tab, tab, tab],
        out_specs=[pl.BlockSpec((1, ts, wq), lambda i, j: (i, j, 0)),
                   pl.BlockSpec((1, ts, wk), lambda i, j: (i, j, 0)), vt_spec],
        out_shape=[jax.ShapeDtypeStruct((b, s, wq), BF16), jax.ShapeDtypeStruct((b, s, wk), BF16), vt_shape],
        compiler_params=_params("parallel", "parallel"),
        name="gqa_prep",
    )(u, u, u, q_gain, k_gain, *tables)


def _mla_prep_body(cq_ref, ckv_ref, ckr_ref, cqg_ref, ckvg_ref, wq_ref, wk_ref, wr_ref, wv_ref,
                   qg_ref, kg_ref, c_ref, s1_ref, s2_ref, qo_ref, ko_ref, vt_ref):
    scale = MLA_QK_DIM ** -0.5 * LOG2E
    cq = _rms(cq_ref[0].astype(F32), cqg_ref[...]).astype(BF16)
    ckv = _rms(ckv_ref[0].astype(F32), ckvg_ref[...]).astype(BF16)
    q = _dot(cq, wq_ref[...])
    k = _dot(ckv, wk_ref[...]) + _dot(ckr_ref[0], wr_ref[...])
    v = _dot(ckv, wv_ref[...])

    def norm_rope(x, g_ref):
        ms = jnp.sum(x * x, axis=-1, keepdims=True) * (1.0 / MLA_QK_DIM)
        y = x * lax.rsqrt(ms + EPS) * g_ref[...]
        return _rope(y, c_ref, s1_ref, s2_ref, MLA_ROPE_DIM // 2)

    for h in range(MLA_HEADS):
        cols = slice(h * MLA_PAD_DIM, (h + 1) * MLA_PAD_DIM)
        qo_ref[0, :, cols] = (norm_rope(q[:, cols], qg_ref) * scale).astype(BF16)
        ko_ref[0, :, cols] = norm_rope(k[:, cols], kg_ref).astype(BF16)
        _store_vt(vt_ref, h, v[:, h * MLA_V_DIM:(h + 1) * MLA_V_DIM])


def mla_prep(u, cq_gain, ckv_gain, w_q, w_k, w_r, w_v, q_gain, k_gain, tables, cq_col, ckv_col, ckr_col, ts):
    b, s, _ = u.shape
    q_rank, kv_rank = w_q.shape[0], w_k.shape[0]
    wqk = MLA_HEADS * MLA_PAD_DIM
    tab = pl.BlockSpec((ts, MLA_PAD_DIM), lambda i, j: (j, 0))
    vt_spec, vt_shape = _vt_spec_shape(b, MLA_HEADS, s, ts, MLA_V_DIM)

    def full(a):
        return pl.BlockSpec(a.shape, lambda i, j: (0,) * a.ndim)

    return pl.pallas_call(
        _mla_prep_body,
        grid=(b, s // ts),
        in_specs=[pl.BlockSpec((1, ts, q_rank), lambda i, j: (i, j, cq_col)),
                  pl.BlockSpec((1, ts, kv_rank), lambda i, j: (i, j, ckv_col)),
                  pl.BlockSpec((1, ts, LANES), lambda i, j: (i, j, ckr_col)),
                  full(cq_gain), full(ckv_gain), full(w_q), full(w_k), full(w_r), full(w_v),
                  full(q_gain), full(k_gain), tab, tab, tab],
        out_specs=[pl.BlockSpec((1, ts, wqk), lambda i, j: (i, j, 0)),
                   pl.BlockSpec((1, ts, wqk), lambda i, j: (i, j, 0)), vt_spec],
        out_shape=[jax.ShapeDtypeStruct((b, s, wqk), BF16), jax.ShapeDtypeStruct((b, s, wqk), BF16), vt_shape],
        compiler_params=_params("parallel", "parallel"),
        name="mla_prep",
    )(u, u, u, cq_gain, ckv_gain, w_q, w_k, w_r, w_v, q_gain, k_gain, *tables)


def _flash_body(q_ref, k_ref, vt_ref, o_ref, sa_scr, sb_scr, acc_scr, *, tk, n_split):
    nk = k_ref.shape[1] // tk
    q = q_ref[0]
    tq = q.shape[0]
    dv = o_ref.shape[2]
    hw = tq // n_split

    def scores(c):
        k = k_ref[0, pl.ds(pl.multiple_of(c * tk, tk), tk), :]
        return lax.dot_general(k, q, (((1,), (1,)), ((), ())), preferred_element_type=F32)

    def softmax_pv(c, s_ref, m):
        m_out = []
        for h in range(n_split):
            cols = slice(h * hw, (h + 1) * hw)
            s = s_ref[:, cols]
            m_new = jnp.maximum(m[h], jnp.max(s, axis=0, keepdims=True))
            alpha = jnp.exp2(m[h] - m_new)
            p = jnp.exp2(s - m_new).astype(BF16)
            acc_scr[:, cols] = alpha * acc_scr[:, cols] + _dot(vt_ref[0, 0, c], p)
            m_out.append(m_new)
        return tuple(m_out)

    def pair(j, m):
        c = 2 * j
        sb_scr[...] = scores(c + 1)
        m = softmax_pv(c, sa_scr, m)
        sa_scr[...] = scores(c + 2)
        return softmax_pv(c + 1, sb_scr, m)

    acc_scr[...] = jnp.zeros(acc_scr.shape, F32)
    sa_scr[...] = scores(0)
    m = tuple(jnp.full((1, hw), -jnp.inf, F32) for _ in range(n_split))
    m = lax.fori_loop(0, nk // 2 - 1, pair, m)
    sb_scr[...] = scores(nk - 1)
    m = softmax_pv(nk - 2, sa_scr, m)
    softmax_pv(nk - 1, sb_scr, m)
    acc = acc_scr[...]
    o_ref[0] = (acc[0:dv, :] / acc[dv:dv + 1, :]).T.astype(o_ref.dtype)


def flash_attention(q, k, vt, n_maps, dk, q_col, k_col, v_head, out_dtype, tq):
    b, s, _ = q.shape
    _, _, nk, dva, tk = vt.shape
    dv = dva - VT_PAD
    return pl.pallas_call(
        functools.partial(_flash_body, tk=tk, n_split=2 if tq % 512 == 0 else 1),
        grid=(b, n_maps, s // tq),
        in_specs=[pl.BlockSpec((1, tq, dk), lambda i, h, a: (i, a, q_col(h))),
                  pl.BlockSpec((1, s, dk), lambda i, h, a: (i, 0, k_col(h))),
                  pl.BlockSpec((1, 1, nk, dva, tk), lambda i, h, a: (i, v_head(h), 0, 0, 0))],
        out_specs=pl.BlockSpec((1, tq, dv), lambda i, h, a: (i, a, h)),
        out_shape=jax.ShapeDtypeStruct((b, s, n_maps * dv), out_dtype),
        scratch_shapes=[pltpu.VMEM((tk, tq), F32), pltpu.VMEM((tk, tq), F32), pltpu.VMEM((dva, tq), F32)],
        compiler_params=_params("parallel", "parallel", "arbitrary"),
        name="flash_attention",
    )(q, k, vt)


def _diff_combine_body(o_ref, lam_ref, g_ref, y_ref, *, lambda_init):
    lam = lam_ref[...]
    lam_full = (jnp.exp(jnp.sum(lam[0:1] * lam[1:2], keepdims=True))
                - jnp.exp(jnp.sum(lam[2:3] * lam[3:4], keepdims=True)) + lambda_init)
    w = DIFF_HEADS * LANES
    for h in range(DIFF_HEADS):
        cols = slice(h * LANES, (h + 1) * LANES)
        d = o_ref[0, :, cols] - lam_full * o_ref[0, :, w + h * LANES:w + (h + 1) * LANES]
        y_ref[0, :, cols] = (_rms(d, g_ref[...]) * (1.0 - lambda_init)).astype(y_ref.dtype)


def diff_combine(o, lam, out_gain, lambda_init, ts):
    b, s, w2 = o.shape
    w = w2 // 2
    return pl.pallas_call(
        functools.partial(_diff_combine_body, lambda_init=lambda_init),
        grid=(b, s // ts),
        in_specs=[pl.BlockSpec((1, ts, w2), lambda i, j: (i, j, 0)),
                  pl.BlockSpec(lam.shape, lambda i, j: (0, 0)),
                  pl.BlockSpec((1, LANES), lambda i, j: (0, 0))],
        out_specs=pl.BlockSpec((1, ts, w), lambda i, j: (i, j, 0)),
        out_shape=jax.ShapeDtypeStruct((b, s, w), BF16),
        compiler_params=_params("parallel", "parallel"),
        name="diff_combine",
    )(o, lam, out_gain)


def _merge_body(xn_ref, ya_ref, yb_ref, yc_ref, yd_ref, wg0, wg1, wg2, wg3, p0, p1, p2, p3, o_ref):
    xn = xn_ref[...]
    acc = None
    for y_ref, wg_ref, p_ref in ((ya_ref, wg0, p0), (yb_ref, wg1, p1), (yc_ref, wg2, p2), (yd_ref, wg3, p3)):
        term = _sigmoid(_dot(xn, wg_ref[...])) * _dot(y_ref[...], p_ref[0])
        acc = term if acc is None else acc + term
    o_ref[...] = acc.astype(o_ref.dtype)


def gated_merge(xn, ys, w_gate, w_branch, tm, tn):
    m, d = xn.shape
    width = w_branch.shape[1]
    nj = d // tn
    y_spec = pl.BlockSpec((tm, width), lambda i, j: (i, 0))
    wg_specs = [pl.BlockSpec((d, tn), functools.partial(lambda i, j, br: (0, br * nj + j), br=br)) for br in range(4)]
    p_specs = [pl.BlockSpec((1, width, tn), functools.partial(lambda i, j, br: (br, 0, j), br=br)) for br in range(4)]
    return pl.pallas_call(
        _merge_body,
        grid=(m // tm, nj),
        in_specs=[pl.BlockSpec((tm, d), lambda i, j: (i, 0)), y_spec, y_spec, y_spec, y_spec] + wg_specs + p_specs,
        out_specs=pl.BlockSpec((tm, tn), lambda i, j: (i, j)),
        out_shape=jax.ShapeDtypeStruct((m, d), BF16),
        compiler_params=_params("parallel", "arbitrary"),
        name="gated_merge",
    )(xn, *ys, w_gate, w_gate, w_gate, w_gate, w_branch, w_branch, w_branch, w_branch)


def _router_body(x_ref, g_ref, r_ref, idx_ref, w_ref, *, n_experts):
    xn = _rms(x_ref[...], g_ref[...])
    logits = jnp.dot(xn, r_ref[...], precision=lax.Precision.HIGHEST, preferred_element_type=F32)
    lane = lax.broadcasted_iota(jnp.int32, logits.shape, 1).astype(F32)
    neg = -jnp.inf
    lg = jnp.where(lane < n_experts, logits, neg)
    v1 = jnp.max(lg, axis=-1, keepdims=True)
    i1 = jnp.min(jnp.where(lg == v1, lane, float(LANES)), axis=-1, keepdims=True)
    lg2 = jnp.where(lane == i1, neg, lg)
    v2 = jnp.max(lg2, axis=-1, keepdims=True)
    i2 = jnp.min(jnp.where(lg2 == v2, lane, float(LANES)), axis=-1, keepdims=True)
    e21 = jnp.exp(v2 - v1)
    w1 = 1.0 / (1.0 + e21)
    w2 = e21 * w1
    idx_ref[...] = jnp.where(lane == 0.0, i1, jnp.where(lane == 1.0, i2, 0.0)).astype(jnp.int32)
    w_ref[...] = jnp.where(lane == 0.0, w1, jnp.where(lane == 1.0, w2, 0.0))


def moe_router(x, g, router_padded, n_experts, tm):
    m, d = x.shape
    return pl.pallas_call(
        functools.partial(_router_body, n_experts=n_experts),
        grid=(m // tm,),
        in_specs=[pl.BlockSpec((tm, d), lambda i: (i, 0)),
                  pl.BlockSpec((1, d), lambda i: (0, 0)),
                  pl.BlockSpec((d, LANES), lambda i: (0, 0))],
        out_specs=[pl.BlockSpec((tm, LANES), lambda i: (i, 0)), pl.BlockSpec((tm, LANES), lambda i: (i, 0))],
        out_shape=[jax.ShapeDtypeStruct((m, LANES), jnp.int32), jax.ShapeDtypeStruct((m, LANES), F32)],
        compiler_params=_params("parallel"),
        name="moe_router",
    )(x, g, router_padded)


def _moe_up_body(te_ref, nv_ref, tok_ref, x_hbm, g_ref, wg_ref, wu_ref, o_ref, xbuf, xn_scr, sem):
    t = pl.program_id(0)
    f = pl.program_id(1)
    tm = xbuf.shape[0]

    def row_copy(r, tok):
        return pltpu.make_async_copy(x_hbm.at[pl.ds(tok, 1)], xbuf.at[pl.ds(r, 1)], sem)

    @pl.when(t < nv_ref[0])
    def _():
        @pl.when(f == 0)
        def _():
            def start(r, c):
                row_copy(r, tok_ref[t * tm + r]).start()
                return c

            def wait(r, c):
                row_copy(r, 0).wait()
                return c

            lax.fori_loop(0, tm, start, 0)
            lax.fori_loop(0, tm, wait, 0)
            xn_scr[...] = _rms(xbuf[...], g_ref[...]).astype(BF16)

        xn = xn_scr[...]
        gate = _dot(xn, wg_ref[0])
        up = _dot(xn, wu_ref[0])
        o_ref[...] = (gate * _sigmoid(gate) * up).astype(o_ref.dtype)

    @pl.when(t >= nv_ref[0])
    def _():
        o_ref[...] = jnp.zeros(o_ref.shape, o_ref.dtype)


def moe_up(tile_expert, n_valid, row_token, x, g, w_gate_up, tm, tf):
    d = x.shape[1]
    f = w_gate_up.shape[2] // 2
    nf = f // tf
    p = row_token.shape[0]

    def wmap(off, t, j, te, nv, tok):
        return (te[t], 0, off + jnp.where(t < nv[0], j, nf - 1))

    grid_spec = pltpu.PrefetchScalarGridSpec(
        num_scalar_prefetch=3,
        grid=(p // tm, nf),
        in_specs=[pl.BlockSpec(memory_space=pl.ANY),
                  pl.BlockSpec((1, d), lambda t, j, te, nv, tok: (0, 0)),
                  pl.BlockSpec((1, d, tf), functools.partial(wmap, 0)),
                  pl.BlockSpec((1, d, tf), functools.partial(wmap, nf))],
        out_specs=pl.BlockSpec((tm, tf), lambda t, j, te, nv, tok: (t, j)),
        scratch_shapes=[pltpu.VMEM((tm, d), F32), pltpu.VMEM((tm, d), BF16), pltpu.SemaphoreType.DMA(())],
    )
    return pl.pallas_call(
        _moe_up_body,
        grid_spec=grid_spec,
        out_shape=jax.ShapeDtypeStruct((p, f), BF16),
        compiler_params=_params("arbitrary", "arbitrary"),
        name="moe_up",
    )(tile_expert, n_valid, row_token, x, g, w_gate_up, w_gate_up)


def _moe_down_body(te_ref, nv_ref, h_ref, w_ref, rw_ref, o_ref):
    t = pl.program_id(0)

    @pl.when(t < nv_ref[0])
    def _():
        o_ref[...] = rw_ref[...] * _dot(h_ref[...], w_ref[0])

    @pl.when(t >= nv_ref[0])
    def _():
        o_ref[...] = jnp.zeros(o_ref.shape, o_ref.dtype)


def moe_down(tile_expert, n_valid, h, w_down, row_w, tm, tn):
    p, f = h.shape
    d = w_down.shape[2]
    nj = d // tn
    grid_spec = pltpu.PrefetchScalarGridSpec(
        num_scalar_prefetch=2,
        grid=(p // tm, nj),
        in_specs=[pl.BlockSpec((tm, f), lambda t, j, te, nv: (t, 0)),
                  pl.BlockSpec((1, f, tn), lambda t, j, te, nv: (te[t], 0, jnp.where(t < nv[0], j, nj - 1))),
                  pl.BlockSpec((tm, 1), lambda t, j, te, nv: (t, 0))],
        out_specs=pl.BlockSpec((tm, tn), lambda t, j, te, nv: (t, j)),
    )
    return pl.pallas_call(
        _moe_down_body,
        grid_spec=grid_spec,
        out_shape=jax.ShapeDtypeStruct((p, d), F32),
        compiler_params=_params("arbitrary", "arbitrary"),
        name="moe_down",
    )(tile_expert, n_valid, h, w_down, row_w)


def _moe_combine_body(pos_ref, x_ref, y_hbm, o_ref, buf0, buf1, sem):
    i = pl.program_id(0)
    tc = buf0.shape[0]

    def row_copy(r, k, src):
        dst = buf0 if k == 0 else buf1
        return pltpu.make_async_copy(y_hbm.at[pl.ds(src, 1)], dst.at[pl.ds(r, 1)], sem)

    def start(r, c):
        for k in range(TOP_K):
            row_copy(r, k, pos_ref[(i * tc + r) * TOP_K + k]).start()
        return c

    def wait(r, c):
        for k in range(TOP_K):
            row_copy(r, k, 0).wait()
        return c

    lax.fori_loop(0, tc, start, 0)
    lax.fori_loop(0, tc, wait, 0)
    o_ref[...] = x_ref[...] + buf0[...] + buf1[...]


def moe_combine(pos, x, y, tc):
    m, d = x.shape
    grid_spec = pltpu.PrefetchScalarGridSpec(
        num_scalar_prefetch=1,
        grid=(m // tc,),
        in_specs=[pl.BlockSpec((tc, d), lambda i, pos: (i, 0)),
                  pl.BlockSpec(memory_space=pl.ANY)],
        out_specs=pl.BlockSpec((tc, d), lambda i, pos: (i, 0)),
        scratch_shapes=[pltpu.VMEM((tc, d), F32), pltpu.VMEM((tc, d), F32), pltpu.SemaphoreType.DMA(())],
    )
    return pl.pallas_call(
        _moe_combine_body,
        grid_spec=grid_spec,
        out_shape=jax.ShapeDtypeStruct((m, d), F32),
        compiler_params=_params("arbitrary"),
        name="moe_combine",
    )(pos, x, y)


def _moe_schedule(idx, wts, n_experts, tm):
    m = idx.shape[0]
    n_assign = m * TOP_K
    e_flat = idx.reshape(n_assign)
    onehot = (e_flat[:, None] == jnp.arange(n_experts, dtype=jnp.int32)[None, :]).astype(jnp.int32)
    csum = jnp.cumsum(onehot, axis=0)
    rank = jnp.take_along_axis(csum, e_flat[:, None], axis=1)[:, 0] - 1
    counts = csum[-1]
    padded = ((counts + tm - 1) // tm) * tm
    ends = jnp.cumsum(padded)
    pos = (ends - padded)[e_flat] + rank
    n_rows = n_assign + n_experts * tm
    row_token = jnp.zeros((n_rows,), jnp.int32).at[pos].set(jnp.arange(n_assign, dtype=jnp.int32) // TOP_K)
    row_w = jnp.zeros((n_rows,), F32).at[pos].set(wts.reshape(n_assign))
    n_tiles = n_rows // tm
    tiles = jnp.arange(n_tiles, dtype=jnp.int32)
    n_valid = (ends[-1] // tm).astype(jnp.int32)
    tile_expert = jnp.sum((tiles[:, None] * tm >= ends[None, :]).astype(jnp.int32), axis=1)
    tile_expert = jnp.minimum(tile_expert, n_experts - 1)
    tile_expert = jnp.where(tiles < n_valid, tile_expert, tile_expert[n_valid - 1])
    return pos.astype(jnp.int32), row_token, row_w.reshape(n_rows, 1), tile_expert, n_valid.reshape(1)


def moe_ffn(x, g, router, w_gate_up, w_down):
    m, d = x.shape
    n_experts = router.shape[1]
    tm = _tile(m, 512)
    router_padded = jnp.pad(router, ((0, 0), (0, LANES - n_experts)))
    idx, wts = moe_router(x, g, router_padded, n_experts, _tile(m, 512))
    pos, row_token, row_w, tile_expert, n_valid = _moe_schedule(idx[:, :TOP_K], wts[:, :TOP_K], n_experts, tm)
    f = w_down.shape[1]
    h = moe_up(tile_expert, n_valid, row_token, x, g, w_gate_up, tm, _tile(f, 512))
    y = moe_down(tile_expert, n_valid, h, w_down, row_w, tm, _tile(d, 512))
    return moe_combine(pos, x, y, _tile(m, 256))


def _rope_cos_sin(pos, dim, theta):
    inv = jnp.power(jnp.float32(theta), -jnp.arange(0, dim, 2, dtype=F32) / dim)
    ang = pos.astype(F32)[:, None] * inv[None, :]
    return jnp.cos(ang), jnp.sin(ang)


def _rope_tables(s):
    pos = jnp.arange(s, dtype=jnp.int32)
    rows = s // GRID_W
    row_pos = jnp.broadcast_to(jnp.arange(rows, dtype=jnp.int32)[:, None], (rows, GRID_W)).reshape(-1)
    col_pos = jnp.broadcast_to(jnp.arange(GRID_W, dtype=jnp.int32)[None, :], (rows, GRID_W)).reshape(-1)

    def ones(n):
        return jnp.ones((s, n), F32)

    def zeros(n):
        return jnp.zeros((s, n), F32)

    cat = functools.partial(jnp.concatenate, axis=1)
    cos, sin = _rope_cos_sin(pos, DIFF_ROPE_DIM, ROPE_THETA)
    rest = DIFF_QK_DIM - DIFF_ROPE_DIM
    half = DIFF_ROPE_DIM // 2
    diff = (cat([cos, cos, ones(rest)] * 2), cat([-sin, zeros(half), zeros(rest)] * 2),
            cat([zeros(half), sin, zeros(rest)] * 2))
    cos, sin = _rope_cos_sin(pos, MLA_ROPE_DIM, ROPE_THETA)
    half = MLA_ROPE_DIM // 2
    tail = MLA_PAD_DIM - MLA_QK_DIM
    mla = (cat([ones(MLA_NOPE_DIM), cos, cos, ones(tail)]),
           cat([zeros(MLA_NOPE_DIM), -sin, zeros(half), zeros(tail)]),
           cat([zeros(MLA_NOPE_DIM), zeros(half), sin, zeros(tail)]))
    cos_r, sin_r = _rope_cos_sin(row_pos, HEAD_DIM // 2, AXIAL_THETA)
    cos_c, sin_c = _rope_cos_sin(col_pos, HEAD_DIM // 2, AXIAL_THETA)
    half = HEAD_DIM // 4
    gqa = (cat([cos_r, cos_r, cos_c, cos_c]), cat([-sin_r, zeros(half), -sin_c, zeros(half)]),
           cat([zeros(half), sin_r, zeros(half), sin_c]))
    return diff, mla, gqa


COL_A_GATE, COL_A_X, COL_B_Q, COL_B_K, COL_B_V, COL_C_Q, COL_D_Q = (i * 512 for i in range(7))
COL_C_KV, COL_D_K, COL_D_V, COL_C_KR, MIX_COLS = 3584, 3840, 4096, 4352, 4480


def _mixer_in_weights(w_in_l):
    widths = (512, 512, 512, 512, 512, 512, 256, 64, 512, 256, 256)
    offs = [0]
    for wd in widths:
        offs.append(offs[-1] + wd)
    (a_gate, a_x, b_q, b_k, b_v, c_q, c_kv, c_kr, d_q, d_k, d_v) = (
        w_in_l[:, offs[i]:offs[i + 1]] for i in range(len(widths)))
    pad = jnp.zeros((w_in_l.shape[0], LANES - MLA_ROPE_DIM), w_in_l.dtype)
    w_mix = jnp.concatenate([a_gate, a_x, b_q, b_k, b_v, c_q, d_q, c_kv, d_k, d_v, c_kr, pad], axis=1)
    return w_mix.astype(BF16), w_in_l[:, offs[-1]:].astype(BF16)


def _mla_weights(w_uq, w_ukv):
    q_rank, kv_rank = w_uq.shape[0], w_ukv.shape[0]
    wq = w_uq.reshape(q_rank, MLA_HEADS, MLA_QK_DIM)
    wq = jnp.pad(wq, ((0, 0), (0, 0), (0, MLA_PAD_DIM - MLA_QK_DIM))).reshape(q_rank, MLA_HEADS * MLA_PAD_DIM)
    wkv = w_ukv.reshape(kv_rank, MLA_HEADS, MLA_NOPE_DIM + MLA_V_DIM)
    wk = jnp.pad(wkv[:, :, :MLA_NOPE_DIM], ((0, 0), (0, 0), (0, MLA_PAD_DIM - MLA_NOPE_DIM)))
    wk = wk.reshape(kv_rank, MLA_HEADS * MLA_PAD_DIM)
    wv = wkv[:, :, MLA_NOPE_DIM:].reshape(kv_rank, MLA_HEADS * MLA_V_DIM)
    place = jnp.zeros((LANES, MLA_HEADS, MLA_PAD_DIM), F32)
    r = jnp.arange(MLA_ROPE_DIM)
    place = place.at[r, :, MLA_NOPE_DIM + r].set(1.0).reshape(LANES, MLA_HEADS * MLA_PAD_DIM)
    return wq.astype(BF16), wk.astype(BF16), place.astype(BF16), wv.astype(BF16)


def _lru_gate_weights(gate_w):
    nb = gate_w.shape[2]
    per = LANES // LRU_BLOCK_DIM
    w = gate_w.reshape(4, nb // per, per, LRU_BLOCK_DIM, LRU_BLOCK_DIM)
    bd = jnp.einsum("dcpkj,pq->cdpkqj", w, jnp.eye(per, dtype=w.dtype))
    return bd.reshape(nb // per, 4, LANES, LANES).astype(BF16)


def _pad_gain(g, width):
    return jnp.pad(g, (0, width - g.shape[0])).reshape(1, width)


def kernel(x, norm_mix, w_in, lru_conv_w, lru_conv_b, lru_gate_w, lru_gate_b, lru_lambda, diff_q_norm, diff_k_norm, diff_lambda, diff_out_norm, mla_cq_norm, mla_ckv_norm, mla_w_uq, mla_w_ukv, mla_q_norm, mla_k_norm, gqa_q_norm, gqa_k_norm, w_branch, w_out, norm_ffn, ffn_w_gate_up, ffn_w_down, moe_router, moe_w_gate_up, moe_w_down):
    b, s, d = x.shape
    m = b * s
    depth = w_in.shape[0]
    lru_width = lru_conv_w.shape[2]
    tables_diff, tables_mla, tables_gqa = _rope_tables(s)
    ts = 512 if s % 1024 == 0 else s // 2
    tq = _tile(s, 1024)
    tm = _tile(m, 512)
    xf = x.reshape(m, d)

    for l in range(depth):
        w_mix, w_gate = _mixer_in_weights(w_in[l])
        u, xn = norm_proj(xf, norm_mix[l].reshape(1, d), w_mix, tm, _tile(MIX_COLS, 640))
        u = u.reshape(b, s, MIX_COLS)

        y_a = lru_mixer(u, lru_conv_w[l], lru_conv_b[l].reshape(1, lru_width), _lru_gate_weights(lru_gate_w[l]),
                        lru_gate_b[l].reshape(4, lru_width), lru_lambda[l], COL_A_GATE // LANES, COL_A_X // LANES)

        lambda_init = 0.8 - 0.6 * math.exp(-0.3 * l)
        q_b, k_b, vt_b = diff_prep(u, jnp.tile(diff_q_norm[l], 2).reshape(1, LANES),
                                   jnp.tile(diff_k_norm[l], 2).reshape(1, LANES),
                                   tables_diff, COL_B_Q // 512, COL_B_K // 512, COL_B_V // 512, ts)
        o_b = flash_attention(q_b, k_b, vt_b, 2 * DIFF_HEADS, LANES,
                              lambda h: h, lambda h: h % DIFF_HEADS, lambda h: h % DIFF_HEADS, F32, tq)
        y_b = diff_combine(o_b, diff_lambda[l], diff_out_norm[l].reshape(1, LANES), lambda_init, ts)

        wq, wk, wr, wv = _mla_weights(mla_w_uq[l], mla_w_ukv[l])
        q_c, k_c, vt_c = mla_prep(u, mla_cq_norm[l].reshape(1, -1), mla_ckv_norm[l].reshape(1, -1), wq, wk, wr, wv,
                                 _pad_gain(mla_q_norm[l], MLA_PAD_DIM), _pad_gain(mla_k_norm[l], MLA_PAD_DIM),
                                 tables_mla, COL_C_Q // 512, COL_C_KV // 256, COL_C_KR // LANES, ts)
        y_c = flash_attention(q_c, k_c, vt_c, MLA_HEADS, MLA_PAD_DIM,
                              lambda h: h, lambda h: h, lambda h: h, BF16, tq)

        group = GQA_Q_HEADS // GQA_KV_HEADS
        q_d, k_d, vt_d = gqa_prep(u, gqa_q_norm[l].reshape(1, LANES), gqa_k_norm[l].reshape(1, LANES), tables_gqa,
                                  COL_D_Q // 512, COL_D_K // 256, COL_D_V // 256, ts)
        y_d = flash_attention(q_d, k_d, vt_d, GQA_Q_HEADS, HEAD_DIM,
                              lambda h: h, lambda h: h // group, lambda h: h // group, BF16, tq)

        ys = [y.reshape(m, -1) for y in (y_a, y_b, y_c, y_d)]
        merged = gated_merge(xn, ys, w_gate, w_branch[l].astype(BF16), tm, _tile(d, 512))
        xf = mm_residual(merged, w_out[l].astype(BF16), xf, tm, _tile(d, 512))

        g_ffn = norm_ffn[l].reshape(1, d)
        if l % 2 == 0:
            f = ffn_w_down.shape[1]
            act = norm_swiglu(xf, g_ffn, ffn_w_gate_up[l // 2].astype(BF16), tm, _tile(f, 512))
            xf = mm_residual(act, ffn_w_down[l // 2].astype(BF16), xf, tm, _tile(d, 512))
        else:
            xf = moe_ffn(xf, g_ffn, moe_router[l // 2], moe_w_gate_up[l // 2].astype(BF16),
                         moe_w_down[l // 2].astype(BF16))
    return xf.reshape(b, s, d)
```

```python
import functools
import math

import jax
import jax.numpy as jnp
from jax import lax
from jax.experimental import pallas as pl
from jax.experimental.pallas import tpu as pltpu

F32 = jnp.float32
BF16 = jnp.bfloat16
EPS = 1e-6
LANES = 128

LRU_BLOCK_DIM = 64
LRU_C = 8.0
DIFF_HEADS = 4
DIFF_QK_DIM = 64
DIFF_ROPE_DIM = 16
MLA_HEADS = 4
MLA_NOPE_DIM = 128
MLA_ROPE_DIM = 64
MLA_QK_DIM = MLA_NOPE_DIM + MLA_ROPE_DIM
MLA_PAD_DIM = 256
MLA_V_DIM = 128
GQA_Q_HEADS = 4
GQA_KV_HEADS = 2
HEAD_DIM = 128
GRID_W = 64
ROPE_THETA = 500000.0
AXIAL_THETA = 10000.0
TOP_K = 2
LOG2E = 1.4426950408889634
VT_PAD = 16

VMEM_LIMIT = 56 * 1024 * 1024


def _params(*sem):
    return pltpu.CompilerParams(dimension_semantics=sem, vmem_limit_bytes=VMEM_LIMIT)


def _tile(dim, pref):
    return pref if dim % pref == 0 else dim


def _sigmoid(x):
    return 1.0 / (1.0 + jnp.exp(-x))


def _dot(a, b):
    return jnp.dot(a, b, preferred_element_type=F32)


def _rms(x, g):
    ms = jnp.mean(x * x, axis=-1, keepdims=True)
    return x * lax.rsqrt(ms + EPS) * g


def _norm_proj_body(x_ref, g_ref, w_ref, o_ref, xn_ref, xn_scr):
    @pl.when(pl.program_id(1) == 0)
    def _():
        xn = _rms(x_ref[...], g_ref[...]).astype(BF16)
        xn_scr[...] = xn
        xn_ref[...] = xn

    o_ref[...] = _dot(xn_scr[...], w_ref[...]).astype(o_ref.dtype)


def norm_proj(x, g, w, tm, tn):
    m, d = x.shape
    n = w.shape[1]
    return pl.pallas_call(
        _norm_proj_body,
        grid=(m // tm, n // tn),
        in_specs=[pl.BlockSpec((tm, d), lambda i, j: (i, 0)),
                  pl.BlockSpec((1, d), lambda i, j: (0, 0)),
                  pl.BlockSpec((d, tn), lambda i, j: (0, j))],
        out_specs=[pl.BlockSpec((tm, tn), lambda i, j: (i, j)),
                   pl.BlockSpec((tm, d), lambda i, j: (i, 0))],
        out_shape=[jax.ShapeDtypeStruct((m, n), BF16), jax.ShapeDtypeStruct((m, d), BF16)],
        scratch_shapes=[pltpu.VMEM((tm, d), BF16)],
        compiler_params=_params("parallel", "arbitrary"),
        name="norm_proj",
    )(x, g, w)


def _norm_swiglu_body(x_ref, g_ref, wg_ref, wu_ref, o_ref, xn_scr):
    @pl.when(pl.program_id(1) == 0)
    def _():
        xn_scr[...] = _rms(x_ref[...], g_ref[...]).astype(BF16)

    xn = xn_scr[...]
    gate = _dot(xn, wg_ref[...])
    up = _dot(xn, wu_ref[...])
    o_ref[...] = (gate * _sigmoid(gate) * up).astype(o_ref.dtype)


def norm_swiglu(x, g, w_gate_up, tm, tf):
    m, d = x.shape
    f = w_gate_up.shape[1] // 2
    nf = f // tf
    return pl.pallas_call(
        _norm_swiglu_body,
        grid=(m // tm, nf),
        in_specs=[pl.BlockSpec((tm, d), lambda i, j: (i, 0)),
                  pl.BlockSpec((1, d), lambda i, j: (0, 0)),
                  pl.BlockSpec((d, tf), lambda i, j: (0, j)),
                  pl.BlockSpec((d, tf), lambda i, j: (0, nf + j))],
        out_specs=pl.BlockSpec((tm, tf), lambda i, j: (i, j)),
        out_shape=jax.ShapeDtypeStruct((m, f), BF16),
        scratch_shapes=[pltpu.VMEM((tm, d), BF16)],
        compiler_params=_params("parallel", "arbitrary"),
        name="norm_swiglu",
    )(x, g, w_gate_up, w_gate_up)


def _mm_residual_body(a_ref, b_ref, r_ref, o_ref):
    o_ref[...] = r_ref[...] + _dot(a_ref[...], b_ref[...])


def mm_residual(a, b, res, tm, tn):
    m, k = a.shape
    n = b.shape[1]
    return pl.pallas_call(
        _mm_residual_body,
        grid=(m // tm, n // tn),
        in_specs=[pl.BlockSpec((tm, k), lambda i, j: (i, 0)),
                  pl.BlockSpec((k, tn), lambda i, j: (0, j)),
                  pl.BlockSpec((tm, tn), lambda i, j: (i, j))],
        out_specs=pl.BlockSpec((tm, tn), lambda i, j: (i, j)),
        out_shape=jax.ShapeDtypeStruct((m, n), F32),
        compiler_params=_params("parallel", "arbitrary"),
        name="mm_residual",
    )(a, b, res)


def _lru_body(ug_ref, ux_ref, cw_ref, cb_ref, gw_ref, gb_ref, lam_ref, o_ref, xpad, hf, *, chunk):
    s = ux_ref.shape[1]
    t = chunk
    nc = s // t
    halo = 8
    xpad[0:halo, :] = jnp.zeros((halo, LANES), F32)
    xpad[s + halo:s + 2 * halo, :] = jnp.zeros((halo, LANES), F32)
    xpad[halo:s + halo, :] = ux_ref[0].astype(F32)

    z = -lam_ref[...]
    softplus = jnp.maximum(z, 0.0) + jnp.log(1.0 + jnp.exp(-jnp.abs(z)))
    decay = -LRU_C * softplus
    row = lax.broadcasted_iota(jnp.int32, (t, LANES), 0)

    def coeffs(c, d):
        ext = xpad[pl.ds(pl.multiple_of(c * t, 8), t + 2 * halo), :]
        xc = cb_ref[...] + sum(cw_ref[j:j + 1, :] * ext[halo - 2 + j:halo - 2 + j + t] for j in range(4))
        xcb = xc.astype(BF16)
        r = _sigmoid(_dot(xcb, gw_ref[0, 2 * d]) + gb_ref[2 * d:2 * d + 1, :])
        i = _sigmoid(_dot(xcb, gw_ref[0, 2 * d + 1]) + gb_ref[2 * d + 1:2 * d + 2, :])
        log_a = decay[d:d + 1, :] * r
        a = jnp.exp(log_a)
        b = jnp.sqrt(1.0 - jnp.exp(2.0 * log_a)) * (i * xc)
        return a, b

    def scan(a, b, reverse):
        k = 1
        while k < t:
            if reverse:
                keep, shift = row < t - k, t - k
            else:
                keep, shift = row >= k, k
            a_s = jnp.where(keep, pltpu.roll(a, shift, 0), 1.0)
            b_s = jnp.where(keep, pltpu.roll(b, shift, 0), 0.0)
            b = a * b_s + b
            a = a * a_s
            k *= 2
        return a, b

    def fwd(c, carry):
        a, b = scan(*coeffs(c, 0), reverse=False)
        h = b + a * carry
        hf[pl.ds(pl.multiple_of(c * t, 8), t), :] = h
        return h[t - 1:t, :]

    lax.fori_loop(0, nc, fwd, jnp.zeros((1, LANES), F32))

    def bwd(i, carry):
        c = nc - 1 - i
        a, b = scan(*coeffs(c, 1), reverse=True)
        h = b + a * carry
        rows = pl.ds(pl.multiple_of(c * t, 16), t)
        g = ug_ref[0, rows, :].astype(F32)
        gelu = 0.5 * g * (1.0 + jnp.tanh(0.7978845608028654 * (g + 0.044715 * (g * g * g))))
        o_ref[0, rows, :] = ((hf[rows, :] + h) * gelu).astype(o_ref.dtype)
        return h[0:1, :]

    lax.fori_loop(0, nc, bwd, jnp.zeros((1, LANES), F32))


def lru_mixer(u, conv_w, conv_b, gate_w_bd, gate_b, lam, gate_col, x_col):
    b, s, _ = u.shape
    w = conv_w.shape[1]
    chunk = _tile(s, 256)
    return pl.pallas_call(
        functools.partial(_lru_body, chunk=chunk),
        grid=(b, w // LANES),
        in_specs=[pl.BlockSpec((1, s, LANES), lambda i, c: (i, 0, gate_col + c)),
                  pl.BlockSpec((1, s, LANES), lambda i, c: (i, 0, x_col + c)),
                  pl.BlockSpec((4, LANES), lambda i, c: (0, c)),
                  pl.BlockSpec((1, LANES), lambda i, c: (0, c)),
                  pl.BlockSpec((1, 4, LANES, LANES), lambda i, c: (c, 0, 0, 0)),
                  pl.BlockSpec((4, LANES), lambda i, c: (0, c)),
                  pl.BlockSpec((2, LANES), lambda i, c: (0, c))],
        out_specs=pl.BlockSpec((1, s, LANES), lambda i, c: (i, 0, c)),
        out_shape=jax.ShapeDtypeStruct((b, s, w), BF16),
        scratch_shapes=[pltpu.VMEM((s + 16, LANES), F32), pltpu.VMEM((s, LANES), F32)],
        compiler_params=_params("parallel", "parallel"),
        name="lru_mixer",
    )(u, u, conv_w, conv_b, gate_w_bd, gate_b, lam)


def _rope(y, c_ref, s1_ref, s2_ref, shift):
    w = y.shape[-1]
    return y * c_ref[...] + pltpu.roll(y, w - shift, 1) * s1_ref[...] + pltpu.roll(y, shift, 1) * s2_ref[...]


def _store_vt(vt_ref, h, v):
    ts, dv = v.shape
    vt_ref[0, h, 0, 0:dv, :] = v.T.astype(BF16)
    vt_ref[0, h, 0, dv:dv + VT_PAD, :] = jnp.ones((VT_PAD, ts), BF16)


def _vt_spec_shape(b, heads, s, ts, dv):
    spec = pl.BlockSpec((1, heads, 1, dv + VT_PAD, ts), lambda i, j: (i, 0, j, 0, 0))
    return spec, jax.ShapeDtypeStruct((b, heads, s // ts, dv + VT_PAD, ts), BF16)


def _diff_prep_body(q_ref, k_ref, v_ref, qg_ref, kg_ref, c_ref, s1_ref, s2_ref, qo_ref, ko_ref, vt_ref):
    lane = lax.broadcasted_iota(jnp.int32, (1, LANES), 1)
    low = lane < DIFF_QK_DIM
    scale = DIFF_QK_DIM ** -0.5 * LOG2E
    nh = DIFF_HEADS

    def norm_rope(x, g_ref):
        x2 = x * x
        ss_lo = jnp.sum(jnp.where(low, x2, 0.0), axis=-1, keepdims=True)
        ss_hi = jnp.sum(jnp.where(low, 0.0, x2), axis=-1, keepdims=True)
        ms = jnp.where(low, ss_lo, ss_hi) * (1.0 / DIFF_QK_DIM)
        y = x * lax.rsqrt(ms + EPS) * g_ref[...]
        return _rope(y, c_ref, s1_ref, s2_ref, DIFF_ROPE_DIM // 2)

    for h in range(nh):
        cols = slice(h * LANES, (h + 1) * LANES)
        q = norm_rope(q_ref[0, :, cols].astype(F32), qg_ref) * scale
        qo_ref[0, :, cols] = jnp.where(low, q, 0.0).astype(BF16)
        qo_ref[0, :, nh * LANES + h * LANES:nh * LANES + (h + 1) * LANES] = jnp.where(low, 0.0, q).astype(BF16)
        ko_ref[0, :, cols] = norm_rope(k_ref[0, :, cols].astype(F32), kg_ref).astype(BF16)
        _store_vt(vt_ref, h, v_ref[0, :, cols].astype(F32))


def diff_prep(u, q_gain, k_gain, tables, q_col, k_col, v_col, ts):
    b, s, _ = u.shape
    w = DIFF_HEADS * LANES
    tab = pl.BlockSpec((ts, LANES), lambda i, j: (j, 0))
    vec = pl.BlockSpec((1, LANES), lambda i, j: (0, 0))
    vt_spec, vt_shape = _vt_spec_shape(b, DIFF_HEADS, s, ts, LANES)
    return pl.pallas_call(
        _diff_prep_body,
        grid=(b, s // ts),
        in_specs=[pl.BlockSpec((1, ts, w), lambda i, j: (i, j, q_col)),
                  pl.BlockSpec((1, ts, w), lambda i, j: (i, j, k_col)),
                  pl.BlockSpec((1, ts, w), lambda i, j: (i, j, v_col)),
                  vec, vec, tab, tab, tab],
        out_specs=[pl.BlockSpec((1, ts, 2 * w), lambda i, j: (i, j, 0)),
                   pl.BlockSpec((1, ts, w), lambda i, j: (i, j, 0)), vt_spec],
        out_shape=[jax.ShapeDtypeStruct((b, s, 2 * w), BF16), jax.ShapeDtypeStruct((b, s, w), BF16), vt_shape],
        compiler_params=_params("parallel", "parallel"),
        name="diff_prep",
    )(u, u, u, q_gain, k_gain, *tables)


def _gqa_prep_body(q_ref, k_ref, v_ref, qg_ref, kg_ref, c_ref, s1_ref, s2_ref, qo_ref, ko_ref, vt_ref):
    scale = HEAD_DIM ** -0.5 * LOG2E
    for h in range(GQA_Q_HEADS):
        cols = slice(h * LANES, (h + 1) * LANES)
        y = _rms(q_ref[0, :, cols].astype(F32), qg_ref[...])
        qo_ref[0, :, cols] = (_rope(y, c_ref, s1_ref, s2_ref, HEAD_DIM // 4) * scale).astype(BF16)
    for h in range(GQA_KV_HEADS):
        cols = slice(h * LANES, (h + 1) * LANES)
        y = _rms(k_ref[0, :, cols].astype(F32), kg_ref[...])
        ko_ref[0, :, cols] = _rope(y, c_ref, s1_ref, s2_ref, HEAD_DIM // 4).astype(BF16)
        _store_vt(vt_ref, h, v_ref[0, :, cols].astype(F32))


def gqa_prep(u, q_gain, k_gain, tables, q_col, k_col, v_col, ts):
    b, s, _ = u.shape
    wq = GQA_Q_HEADS * HEAD_DIM
    wk = GQA_KV_HEADS * HEAD_DIM
    tab = pl.BlockSpec((ts, LANES), lambda i, j: (j, 0))
    vec = pl.BlockSpec((1, LANES), lambda i, j: (0, 0))
    vt_spec, vt_shape = _vt_spec_shape(b, GQA_KV_HEADS, s, ts, HEAD_DIM)
    return pl.pallas_call(
        _gqa_prep_body,
        grid=(b, s // ts),
        in_specs=[pl.BlockSpec((1, ts, wq), lambda i, j: (i, j, q_col)),
                  pl.BlockSpec((1, ts, wk), lambda i, j: (i, j, k_col)),
                  pl.BlockSpec((1, ts, wk), lambda i, j: (i, j, v_col)),
                  vec, vec, tab, tab, tab],
        out_specs=[pl.BlockSpec((1, ts, wq), lambda i, j: (i, j, 0)),
                   pl.BlockSpec((1, ts, wk), lambda i, j: (i, j, 0)), vt_spec],
        out_shape=[jax.ShapeDtypeStruct((b, s, wq), BF16), jax.ShapeDtypeStruct((b, s, wk), BF16), vt_shape],
        compiler_params=_params("parallel", "parallel"),
        name="gqa_prep",
    )(u, u, u, q_gain, k_gain, *tables)


def _mla_prep_body(cq_ref, ckv_ref, ckr_ref, cqg_ref, ckvg_ref, wq_ref, wk_ref, wr_ref, wv_ref,
                   qg_ref, kg_ref, c_ref, s1_ref, s2_ref, qo_ref, ko_ref, vt_ref):
    scale = MLA_QK_DIM ** -0.5 * LOG2E
    cq = _rms(cq_ref[0].astype(F32), cqg_ref[...]).astype(BF16)
    ckv = _rms(ckv_ref[0].astype(F32), ckvg_ref[...]).astype(BF16)
    q = _dot(cq, wq_ref[...])
    k = _dot(ckv, wk_ref[...]) + _dot(ckr_ref[0], wr_ref[...])
    v = _dot(ckv, wv_ref[...])

    def norm_rope(x, g_ref):
        ms = jnp.sum(x * x, axis=-1, keepdims=True) * (1.0 / MLA_QK_DIM)
        y = x * lax.rsqrt(ms + EPS) * g_ref[...]
        return _rope(y, c_ref, s1_ref, s2_ref, MLA_ROPE_DIM // 2)

    for h in range(MLA_HEADS):
        cols = slice(h * MLA_PAD_DIM, (h + 1) * MLA_PAD_DIM)
        qo_ref[0, :, cols] = (norm_rope(q[:, cols], qg_ref) * scale).astype(BF16)
        ko_ref[0, :, cols] = norm_rope(k[:, cols], kg_ref).astype(BF16)
        _store_vt(vt_ref, h, v[:, h * MLA_V_DIM:(h + 1) * MLA_V_DIM])


def mla_prep(u, cq_gain, ckv_gain, w_q, w_k, w_r, w_v, q_gain, k_gain, tables, cq_col, ckv_col, ckr_col, ts):
    b, s, _ = u.shape
    q_rank, kv_rank = w_q.shape[0], w_k.shape[0]
    wqk = MLA_HEADS * MLA_PAD_DIM
    tab = pl.BlockSpec((ts, MLA_PAD_DIM), lambda i, j: (j, 0))
    vt_spec, vt_shape = _vt_spec_shape(b, MLA_HEADS, s, ts, MLA_V_DIM)

    def full(a):
        return pl.BlockSpec(a.shape, lambda i, j: (0,) * a.ndim)

    return pl.pallas_call(
        _mla_prep_body,
        grid=(b, s // ts),
        in_specs=[pl.BlockSpec((1, ts, q_rank), lambda i, j: (i, j, cq_col)),
                  pl.BlockSpec((1, ts, kv_rank), lambda i, j: (i, j, ckv_col)),
                  pl.BlockSpec((1, ts, LANES), lambda i, j: (i, j, ckr_col)),
                  full(cq_gain), full(ckv_gain), full(w_q), full(w_k), full(w_r), full(w_v),
                  full(q_gain), full(k_gain), tab, tab, tab],
        out_specs=[pl.BlockSpec((1, ts, wqk), lambda i, j: (i, j, 0)),
                   pl.BlockSpec((1, ts, wqk), lambda i, j: (i, j, 0)), vt_spec],
        out_shape=[jax.ShapeDtypeStruct((b, s, wqk), BF16), jax.ShapeDtypeStruct((b, s, wqk), BF16), vt_shape],
        compiler_params=_params("parallel", "parallel"),
        name="mla_prep",
    )(u, u, u, cq_gain, ckv_gain, w_q, w_k, w_r, w_v, q_gain, k_gain, *tables)


def _flash_body(q_ref, k_ref, vt_ref, o_ref, sa_scr, sb_scr, acc_scr, *, tk, n_split):
    nk = k_ref.shape[1] // tk
    q = q_ref[0]
    tq = q.shape[0]
    dv = o_ref.shape[2]
    hw = tq // n_split

    def scores(c):
        k = k_ref[0, pl.ds(pl.multiple_of(c * tk, tk), tk), :]
        return lax.dot_general(k, q, (((1,), (1,)), ((), ())), preferred_element_type=F32)

    def softmax_pv(c, s_ref, m):
        m_out = []
        for h in range(n_split):
            cols = slice(h * hw, (h + 1) * hw)
            s = s_ref[:, cols]
            m_new = jnp.maximum(m[h], jnp.max(s, axis=0, keepdims=True))
            alpha = jnp.exp2(m[h] - m_new)
            p = jnp.exp2(s - m_new).astype(BF16)
            acc_scr[:, cols] = alpha * acc_scr[:, cols] + _dot(vt_ref[0, 0, c], p)
            m_out.append(m_new)
        return tuple(m_out)

    def pair(j, m):
        c = 2 * j
        sb_scr[...] = scores(c + 1)
        m = softmax_pv(c, sa_scr, m)
        sa_scr[...] = scores(c + 2)
        return softmax_pv(c + 1, sb_scr, m)

    acc_scr[...] = jnp.zeros(acc_scr.shape, F32)
    sa_scr[...] = scores(0)
    m = tuple(jnp.full((1, hw), -jnp.inf, F32) for _ in range(n_split))
    m = lax.fori_loop(0, nk // 2 - 1, pair, m)
    sb_scr[...] = scores(nk - 1)
    m = softmax_pv(nk - 2, sa_scr, m)
    softmax_pv(nk - 1, sb_scr, m)
    acc = acc_scr[...]
    o_ref[0] = (acc[0:dv, :] / acc[dv:dv + 1, :]).T.astype(o_ref.dtype)


def flash_attention(q, k, vt, n_maps, dk, q_col, k_col, v_head, out_dtype, tq):
    b, s, _ = q.shape
    _, _, nk, dva, tk = vt.shape
    dv = dva - VT_PAD
    return pl.pallas_call(
        functools.partial(_flash_body, tk=tk, n_split=2 if tq % 512 == 0 else 1),
        grid=(b, n_maps, s // tq),
        in_specs=[pl.BlockSpec((1, tq, dk), lambda i, h, a: (i, a, q_col(h))),
                  pl.BlockSpec((1, s, dk), lambda i, h, a: (i, 0, k_col(h))),
                  pl.BlockSpec((1, 1, nk, dva, tk), lambda i, h, a: (i, v_head(h), 0, 0, 0))],
        out_specs=pl.BlockSpec((1, tq, dv), lambda i, h, a: (i, a, h)),
        out_shape=jax.ShapeDtypeStruct((b, s, n_maps * dv), out_dtype),
        scratch_shapes=[pltpu.VMEM((tk, tq), F32), pltpu.VMEM((tk, tq), F32), pltpu.VMEM((dva, tq), F32)],
        compiler_params=_params("parallel", "parallel", "arbitrary"),
        name="flash_attention",
    )(q, k, vt)


def _diff_combine_body(o_ref, lam_ref, g_ref, y_ref, *, lambda_init):
    lam = lam_ref[...]
    lam_full = (jnp.exp(jnp.sum(lam[0:1] * lam[1:2], keepdims=True))
                - jnp.exp(jnp.sum(lam[2:3] * lam[3:4], keepdims=True)) + lambda_init)
    w = DIFF_HEADS * LANES
    for h in range(DIFF_HEADS):
        cols = slice(h * LANES, (h + 1) * LANES)
        d = o_ref[0, :, cols] - lam_full * o_ref[0, :, w + h * LANES:w + (h + 1) * LANES]
        y_ref[0, :, cols] = (_rms(d, g_ref[...]) * (1.0 - lambda_init)).astype(y_ref.dtype)


def diff_combine(o, lam, out_gain, lambda_init, ts):
    b, s, w2 = o.shape
    w = w2 // 2
    return pl.pallas_call(
        functools.partial(_diff_combine_body, lambda_init=lambda_init),
        grid=(b, s // ts),
        in_specs=[pl.BlockSpec((1, ts, w2), lambda i, j: (i, j, 0)),
                  pl.BlockSpec(lam.shape, lambda i, j: (0, 0)),
                  pl.BlockSpec((1, LANES), lambda i, j: (0, 0))],
        out_specs=pl.BlockSpec((1, ts, w), lambda i, j: (i, j, 0)),
        out_shape=jax.ShapeDtypeStruct((b, s, w), BF16),
        compiler_params=_params("parallel", "parallel"),
        name="diff_combine",
    )(o, lam, out_gain)


def _merge_body(xn_ref, ya_ref, yb_ref, yc_ref, yd_ref, wg0, wg1, wg2, wg3, p0, p1, p2, p3, o_ref):
    xn = xn_ref[...]
    acc = None
    for y_ref, wg_ref, p_ref in ((ya_ref, wg0, p0), (yb_ref, wg1, p1), (yc_ref, wg2, p2), (yd_ref, wg3, p3)):
        term = _sigmoid(_dot(xn, wg_ref[...])) * _dot(y_ref[...], p_ref[0])
        acc = term if acc is None else acc + term
    o_ref[...] = acc.astype(o_ref.dtype)


def gated_merge(xn, ys, w_gate, w_branch, tm, tn):
    m, d = xn.shape
    width = w_branch.shape[1]
    nj = d // tn
    y_spec = pl.BlockSpec((tm, width), lambda i, j: (i, 0))
    wg_specs = [pl.BlockSpec((d, tn), functools.partial(lambda i, j, br: (0, br * nj + j), br=br)) for br in range(4)]
    p_specs = [pl.BlockSpec((1, width, tn), functools.partial(lambda i, j, br: (br, 0, j), br=br)) for br in range(4)]
    return pl.pallas_call(
        _merge_body,
        grid=(m // tm, nj),
        in_specs=[pl.BlockSpec((tm, d), lambda i, j: (i, 0)), y_spec, y_spec, y_spec, y_spec] + wg_specs + p_specs,
        out_specs=pl.BlockSpec((tm, tn), lambda i, j: (i, j)),
        out_shape=jax.ShapeDtypeStruct((m, d), BF16),
        compiler_params=_params("parallel", "arbitrary"),
        name="gated_merge",
    )(xn, *ys, w_gate, w_gate, w_gate, w_gate, w_branch, w_branch, w_branch, w_branch)


def _router_body(x_ref, g_ref, r_ref, idx_ref, w_ref, *, n_experts):
    xn = _rms(x_ref[...], g_ref[...])
    logits = jnp.dot(xn, r_ref[...], precision=lax.Precision.HIGHEST, preferred_element_type=F32)
    lane = lax.broadcasted_iota(jnp.int32, logits.shape, 1).astype(F32)
    neg = -jnp.inf
    lg = jnp.where(lane < n_experts, logits, neg)
    v1 = jnp.max(lg, axis=-1, keepdims=True)
    i1 = jnp.min(jnp.where(lg == v1, lane, float(LANES)), axis=-1, keepdims=True)
    lg2 = jnp.where(lane == i1, neg, lg)
    v2 = jnp.max(lg2, axis=-1, keepdims=True)
    i2 = jnp.min(jnp.where(lg2 == v2, lane, float(LANES)), axis=-1, keepdims=True)
    e21 = jnp.exp(v2 - v1)
    w1 = 1.0 / (1.0 + e21)
    w2 = e21 * w1
    idx_ref[...] = jnp.where(lane == 0.0, i1, jnp.where(lane == 1.0, i2, 0.0)).astype(jnp.int32)
    w_ref[...] = jnp.where(lane == 0.0, w1, jnp.where(lane == 1.0, w2, 0.0))


def moe_router(x, g, router_padded, n_experts, tm):
    m, d = x.shape
    return pl.pallas_call(
        functools.partial(_router_body, n_experts=n_experts),
        grid=(m // tm,),
        in_specs=[pl.BlockSpec((tm, d), lambda i: (i, 0)),
                  pl.BlockSpec((1, d), lambda i: (0, 0)),
                  pl.BlockSpec((d, LANES), lambda i: (0, 0))],
        out_specs=[pl.BlockSpec((tm, LANES), lambda i: (i, 0)), pl.BlockSpec((tm, LANES), lambda i: (i, 0))],
        out_shape=[jax.ShapeDtypeStruct((m, LANES), jnp.int32), jax.ShapeDtypeStruct((m, LANES), F32)],
        compiler_params=_params("parallel"),
        name="moe_router",
    )(x, g, router_padded)


def _moe_up_body(te_ref, nv_ref, tok_ref, x_hbm, g_ref, wg_ref, wu_ref, o_ref, xbuf, xn_scr, sem, *, rows_per_step):
    t = pl.program_id(0)
    f = pl.program_id(1)
    tm = xn_scr.shape[0]
    nt = pl.num_programs(0)
    nf = pl.num_programs(1)
    n_rows = rows_per_step * nf
    slot = t % 2

    def row_copy(dst_slot, r, tok):
        return pltpu.make_async_copy(x_hbm.at[pl.ds(tok, 1)], xbuf.at[dst_slot, pl.ds(r, 1)], sem.at[dst_slot])

    def prefetch_next_tile():
        for i in range(rows_per_step):
            r = f * rows_per_step + i
            row_copy(1 - slot, r, tok_ref[(t + 1) * tm + r]).start()

    def wait_rows(dst_slot):
        def wait(r, c):
            row_copy(dst_slot, r, 0).wait()
            return c
        lax.fori_loop(0, n_rows, wait, 0)

    @pl.when(jnp.logical_and(t == 0, f == 0))
    def _():
        def start(r, c):
            row_copy(0, r, tok_ref[r]).start()
            return c
        lax.fori_loop(0, n_rows, start, 0)

    @pl.when(f == 0)
    def _():
        wait_rows(slot)

    @pl.when(t < nv_ref[0])
    def _():
        @pl.when(f == 0)
        def _():
            xn_scr[...] = _rms(xbuf[slot, 0:tm, :], g_ref[...]).astype(BF16)

        prefetch_next_tile()
        xn = xn_scr[...]
        gate = _dot(xn, wg_ref[0])
        up = _dot(xn, wu_ref[0])
        o_ref[...] = (gate * _sigmoid(gate) * up).astype(o_ref.dtype)

    @pl.when(t >= nv_ref[0])
    def _():
        prefetch_next_tile()
        o_ref[...] = jnp.zeros(o_ref.shape, o_ref.dtype)

    @pl.when(jnp.logical_and(t == nt - 1, f == nf - 1))
    def _():
        wait_rows(1 - slot)


def moe_up(tile_expert, n_valid, row_token, x, g, w_gate_up, tm, tf):
    d = x.shape[1]
    f = w_gate_up.shape[2] // 2
    nf = f // tf
    p = row_token.shape[0] - 2 * tm
    rows_per_step = -(-tm // nf)
    buf_rows = -(-rows_per_step * nf // 8) * 8

    def wmap(off, t, j, te, nv, tok):
        return (te[t], 0, off + jnp.where(t < nv[0], j, nf - 1))

    grid_spec = pltpu.PrefetchScalarGridSpec(
        num_scalar_prefetch=3,
        grid=(p // tm, nf),
        in_specs=[pl.BlockSpec(memory_space=pl.ANY),
                  pl.BlockSpec((1, d), lambda t, j, te, nv, tok: (0, 0)),
                  pl.BlockSpec((1, d, tf), functools.partial(wmap, 0)),
                  pl.BlockSpec((1, d, tf), functools.partial(wmap, nf))],
        out_specs=pl.BlockSpec((tm, tf), lambda t, j, te, nv, tok: (t, j)),
        scratch_shapes=[pltpu.VMEM((2, buf_rows, d), F32), pltpu.VMEM((tm, d), BF16), pltpu.SemaphoreType.DMA((2,))],
    )
    return pl.pallas_call(
        functools.partial(_moe_up_body, rows_per_step=rows_per_step),
        grid_spec=grid_spec,
        out_shape=jax.ShapeDtypeStruct((p, f), BF16),
        compiler_params=_params("arbitrary", "arbitrary"),
        name="moe_up",
    )(tile_expert, n_valid, row_token, x, g, w_gate_up, w_gate_up)


def _moe_down_body(te_ref, nv_ref, h_ref, w_ref, o_ref):
    t = pl.program_id(0)

    @pl.when(t < nv_ref[0])
    def _():
        o_ref[...] = _dot(h_ref[...], w_ref[0])

    @pl.when(t >= nv_ref[0])
    def _():
        o_ref[...] = jnp.zeros(o_ref.shape, o_ref.dtype)


def moe_down(tile_expert, n_valid, h, w_down, tm, tn):
    p, f = h.shape
    d = w_down.shape[2]
    nj = d // tn
    grid_spec = pltpu.PrefetchScalarGridSpec(
        num_scalar_prefetch=2,
        grid=(p // tm, nj),
        in_specs=[pl.BlockSpec((tm, f), lambda t, j, te, nv: (t, 0)),
                  pl.BlockSpec((1, f, tn), lambda t, j, te, nv: (te[t], 0, jnp.where(t < nv[0], j, nj - 1)))],
        out_specs=pl.BlockSpec((tm, tn), lambda t, j, te, nv: (t, j)),
    )
    return pl.pallas_call(
        _moe_down_body,
        grid_spec=grid_spec,
        out_shape=jax.ShapeDtypeStruct((p, d), F32),
        compiler_params=_params("arbitrary", "arbitrary"),
        name="moe_down",
    )(tile_expert, n_valid, h, w_down)


def _moe_combine_body(pos_ref, x_ref, w_ref, y_hbm, o_ref, buf0, buf1, sem):
    i = pl.program_id(0)
    tc = buf0.shape[0]

    def row_copy(r, k, src):
        dst = buf0 if k == 0 else buf1
        return pltpu.make_async_copy(y_hbm.at[pl.ds(src, 1)], dst.at[pl.ds(r, 1)], sem)

    def start(r, c):
        for k in range(TOP_K):
            row_copy(r, k, pos_ref[(i * tc + r) * TOP_K + k]).start()
        return c

    def wait(r, c):
        for k in range(TOP_K):
            row_copy(r, k, 0).wait()
        return c

    lax.fori_loop(0, tc, start, 0)
    lax.fori_loop(0, tc, wait, 0)
    o_ref[...] = x_ref[...] + w_ref[:, 0:1] * buf0[...] + w_ref[:, 1:2] * buf1[...]


def moe_combine(pos, x, wts, y, tc):
    m, d = x.shape
    grid_spec = pltpu.PrefetchScalarGridSpec(
        num_scalar_prefetch=1,
        grid=(m // tc,),
        in_specs=[pl.BlockSpec((tc, d), lambda i, pos: (i, 0)),
                  pl.BlockSpec((tc, LANES), lambda i, pos: (i, 0)),
                  pl.BlockSpec(memory_space=pl.ANY)],
        out_specs=pl.BlockSpec((tc, d), lambda i, pos: (i, 0)),
        scratch_shapes=[pltpu.VMEM((tc, d), F32), pltpu.VMEM((tc, d), F32), pltpu.SemaphoreType.DMA(())],
    )
    return pl.pallas_call(
        _moe_combine_body,
        grid_spec=grid_spec,
        out_shape=jax.ShapeDtypeStruct((m, d), F32),
        compiler_params=_params("arbitrary"),
        name="moe_combine",
    )(pos, x, wts, y)


def _moe_schedule(idx, n_experts, tm):
    m = idx.shape[0]
    n_assign = m * TOP_K
    e_flat = idx.reshape(n_assign)
    onehot = (e_flat[:, None] == jnp.arange(n_experts, dtype=jnp.int32)[None, :]).astype(jnp.int32)
    csum = jnp.cumsum(onehot, axis=0)
    rank = jnp.take_along_axis(csum, e_flat[:, None], axis=1)[:, 0] - 1
    counts = csum[-1]
    padded = ((counts + tm - 1) // tm) * tm
    ends = jnp.cumsum(padded)
    pos = (ends - padded)[e_flat] + rank
    n_rows = n_assign + n_experts * tm
    row_token = jnp.zeros((n_rows + 2 * tm,), jnp.int32).at[pos].set(jnp.arange(n_assign, dtype=jnp.int32) // TOP_K)
    n_tiles = n_rows // tm
    tiles = jnp.arange(n_tiles, dtype=jnp.int32)
    n_valid = (ends[-1] // tm).astype(jnp.int32)
    tile_expert = jnp.sum((tiles[:, None] * tm >= ends[None, :]).astype(jnp.int32), axis=1)
    tile_expert = jnp.minimum(tile_expert, n_experts - 1)
    tile_expert = jnp.where(tiles < n_valid, tile_expert, tile_expert[n_valid - 1])
    return pos.astype(jnp.int32), row_token, tile_expert, n_valid.reshape(1)


def moe_ffn(x, g, router, w_gate_up, w_down):
    m, d = x.shape
    n_experts = router.shape[1]
    tm = _tile(m, 512)
    router_padded = jnp.pad(router, ((0, 0), (0, LANES - n_experts)))
    idx, wts = moe_router(x, g, router_padded, n_experts, _tile(m, 512))
    pos, row_token, tile_expert, n_valid = _moe_schedule(idx[:, :TOP_K], n_experts, tm)
    f = w_down.shape[1]
    h = moe_up(tile_expert, n_valid, row_token, x, g, w_gate_up, tm, _tile(f, 1024) if f % 1024 == 0 else _tile(f, 512))
    y = moe_down(tile_expert, n_valid, h, w_down, tm, _tile(d, 512))
    return moe_combine(pos, x, wts, y, _tile(m, 256))


def _rope_cos_sin(pos, dim, theta):
    inv = jnp.power(jnp.float32(theta), -jnp.arange(0, dim, 2, dtype=F32) / dim)
    ang = pos.astype(F32)[:, None] * inv[None, :]
    return jnp.cos(ang), jnp.sin(ang)


def _rope_tables(s):
    pos = jnp.arange(s, dtype=jnp.int32)
    rows = s // GRID_W
    row_pos = jnp.broadcast_to(jnp.arange(rows, dtype=jnp.int32)[:, None], (rows, GRID_W)).reshape(-1)
    col_pos = jnp.broadcast_to(jnp.arange(GRID_W, dtype=jnp.int32)[None, :], (rows, GRID_W)).reshape(-1)

    def ones(n):
        return jnp.ones((s, n), F32)

    def zeros(n):
        return jnp.zeros((s, n), F32)

    cat = functools.partial(jnp.concatenate, axis=1)
    cos, sin = _rope_cos_sin(pos, DIFF_ROPE_DIM, ROPE_THETA)
    rest = DIFF_QK_DIM - DIFF_ROPE_DIM
    half = DIFF_ROPE_DIM // 2
    diff = (cat([cos, cos, ones(rest)] * 2), cat([-sin, zeros(half), zeros(rest)] * 2),
            cat([zeros(half), sin, zeros(rest)] * 2))
    cos, sin = _rope_cos_sin(pos, MLA_ROPE_DIM, ROPE_THETA)
    half = MLA_ROPE_DIM // 2
    tail = MLA_PAD_DIM - MLA_QK_DIM
    mla = (cat([ones(MLA_NOPE_DIM), cos, cos, ones(tail)]),
           cat([zeros(MLA_NOPE_DIM), -sin, zeros(half), zeros(tail)]),
           cat([zeros(MLA_NOPE_DIM), zeros(half), sin, zeros(tail)]))
    cos_r, sin_r = _rope_cos_sin(row_pos, HEAD_DIM // 2, AXIAL_THETA)
    cos_c, sin_c = _rope_cos_sin(col_pos, HEAD_DIM // 2, AXIAL_THETA)
    half = HEAD_DIM // 4
    gqa = (cat([cos_r, cos_r, cos_c, cos_c]), cat([-sin_r, zeros(half), -sin_c, zeros(half)]),
           cat([zeros(half), sin_r, zeros(half), sin_c]))
    return diff, mla, gqa


COL_A_GATE, COL_A_X, COL_B_Q, COL_B_K, COL_B_V, COL_C_Q, COL_D_Q = (i * 512 for i in range(7))
COL_C_KV, COL_D_K, COL_D_V, COL_C_KR, MIX_COLS = 3584, 3840, 4096, 4352, 4480


def _mixer_in_weights(w_in_l):
    widths = (512, 512, 512, 512, 512, 512, 256, 64, 512, 256, 256)
    offs = [0]
    for wd in widths:
        offs.append(offs[-1] + wd)
    (a_gate, a_x, b_q, b_k, b_v, c_q, c_kv, c_kr, d_q, d_k, d_v) = (
        w_in_l[:, offs[i]:offs[i + 1]] for i in range(len(widths)))
    pad = jnp.zeros((w_in_l.shape[0], LANES - MLA_ROPE_DIM), w_in_l.dtype)
    w_mix = jnp.concatenate([a_gate, a_x, b_q, b_k, b_v, c_q, d_q, c_kv, d_k, d_v, c_kr, pad], axis=1)
    return w_mix.astype(BF16), w_in_l[:, offs[-1]:].astype(BF16)


def _mla_weights(w_uq, w_ukv):
    q_rank, kv_rank = w_uq.shape[0], w_ukv.shape[0]
    wq = w_uq.reshape(q_rank, MLA_HEADS, MLA_QK_DIM)
    wq = jnp.pad(wq, ((0, 0), (0, 0), (0, MLA_PAD_DIM - MLA_QK_DIM))).reshape(q_rank, MLA_HEADS * MLA_PAD_DIM)
    wkv = w_ukv.reshape(kv_rank, MLA_HEADS, MLA_NOPE_DIM + MLA_V_DIM)
    wk = jnp.pad(wkv[:, :, :MLA_NOPE_DIM], ((0, 0), (0, 0), (0, MLA_PAD_DIM - MLA_NOPE_DIM)))
    wk = wk.reshape(kv_rank, MLA_HEADS * MLA_PAD_DIM)
    wv = wkv[:, :, MLA_NOPE_DIM:].reshape(kv_rank, MLA_HEADS * MLA_V_DIM)
    place = jnp.zeros((LANES, MLA_HEADS, MLA_PAD_DIM), F32)
    r = jnp.arange(MLA_ROPE_DIM)
    place = place.at[r, :, MLA_NOPE_DIM + r].set(1.0).reshape(LANES, MLA_HEADS * MLA_PAD_DIM)
    return wq.astype(BF16), wk.astype(BF16), place.astype(BF16), wv.astype(BF16)


def _lru_gate_weights(gate_w):
    nb = gate_w.shape[2]
    per = LANES // LRU_BLOCK_DIM
    w = gate_w.reshape(4, nb // per, per, LRU_BLOCK_DIM, LRU_BLOCK_DIM)
    bd = jnp.einsum("dcpkj,pq->cdpkqj", w, jnp.eye(per, dtype=w.dtype))
    return bd.reshape(nb // per, 4, LANES, LANES).astype(BF16)


def _pad_gain(g, width):
    return jnp.pad(g, (0, width - g.shape[0])).reshape(1, width)


def kernel(x, norm_mix, w_in, lru_conv_w, lru_conv_b, lru_gate_w, lru_gate_b, lru_lambda, diff_q_norm, diff_k_norm, diff_lambda, diff_out_norm, mla_cq_norm, mla_ckv_norm, mla_w_uq, mla_w_ukv, mla_q_norm, mla_k_norm, gqa_q_norm, gqa_k_norm, w_branch, w_out, norm_ffn, ffn_w_gate_up, ffn_w_down, moe_router, moe_w_gate_up, moe_w_down):
    b, s, d = x.shape
    m = b * s
    depth = w_in.shape[0]
    lru_width = lru_conv_w.shape[2]
    tables_diff, tables_mla, tables_gqa = _rope_tables(s)
    ts = 512 if s % 1024 == 0 else s // 2
    tq = _tile(s, 1024)
    tm = _tile(m, 512)
    tm_big = _tile(m, 1024)
    xf = x.reshape(m, d)

    for l in range(depth):
        w_mix, w_gate = _mixer_in_weights(w_in[l])
        u, xn = norm_proj(xf, norm_mix[l].reshape(1, d), w_mix, tm, _tile(MIX_COLS, 640))
        u = u.reshape(b, s, MIX_COLS)

        y_a = lru_mixer(u, lru_conv_w[l], lru_conv_b[l].reshape(1, lru_width), _lru_gate_weights(lru_gate_w[l]),
                        lru_gate_b[l].reshape(4, lru_width), lru_lambda[l], COL_A_GATE // LANES, COL_A_X // LANES)

        lambda_init = 0.8 - 0.6 * math.exp(-0.3 * l)
        q_b, k_b, vt_b = diff_prep(u, jnp.tile(diff_q_norm[l], 2).reshape(1, LANES),
                                   jnp.tile(diff_k_norm[l], 2).reshape(1, LANES),
                                   tables_diff, COL_B_Q // 512, COL_B_K // 512, COL_B_V // 512, ts)
        o_b = flash_attention(q_b, k_b, vt_b, 2 * DIFF_HEADS, LANES,
                              lambda h: h, lambda h: h % DIFF_HEADS, lambda h: h % DIFF_HEADS, F32, tq)
        y_b = diff_combine(o_b, diff_lambda[l], diff_out_norm[l].reshape(1, LANES), lambda_init, ts)

        wq, wk, wr, wv = _mla_weights(mla_w_uq[l], mla_w_ukv[l])
        q_c, k_c, vt_c = mla_prep(u, mla_cq_norm[l].reshape(1, -1), mla_ckv_norm[l].reshape(1, -1), wq, wk, wr, wv,
                                 _pad_gain(mla_q_norm[l], MLA_PAD_DIM), _pad_gain(mla_k_norm[l], MLA_PAD_DIM),
                                 tables_mla, COL_C_Q // 512, COL_C_KV // 256, COL_C_KR // LANES, ts)
        y_c = flash_attention(q_c, k_c, vt_c, MLA_HEADS, MLA_PAD_DIM,
                              lambda h: h, lambda h: h, lambda h: h, BF16, tq)

        group = GQA_Q_HEADS // GQA_KV_HEADS
        q_d, k_d, vt_d = gqa_prep(u, gqa_q_norm[l].reshape(1, LANES), gqa_k_norm[l].reshape(1, LANES), tables_gqa,
                                  COL_D_Q // 512, COL_D_K // 256, COL_D_V // 256, ts)
        y_d = flash_attention(q_d, k_d, vt_d, GQA_Q_HEADS, HEAD_DIM,
                              lambda h: h, lambda h: h // group, lambda h: h // group, BF16, tq)

        ys = [y.reshape(m, -1) for y in (y_a, y_b, y_c, y_d)]
        merged = gated_merge(xn, ys, w_gate, w_branch[l].astype(BF16), tm, _tile(d, 512))
        xf = mm_residual(merged, w_out[l].astype(BF16), xf, tm_big, _tile(d, 512))

        g_ffn = norm_ffn[l].reshape(1, d)
        if l % 2 == 0:
            f = ffn_w_down.shape[1]
            act = norm_swiglu(xf, g_ffn, ffn_w_gate_up[l // 2].astype(BF16), tm_big, _tile(f, 512))
            xf = mm_residual(act, ffn_w_down[l // 2].astype(BF16), xf, tm_big, _tile(d, 512))
        else:
            xf = moe_ffn(xf, g_ffn, moe_router[l // 2], moe_w_gate_up[l // 2].astype(BF16),
                         moe_w_down[l // 2].astype(BF16))
    return xf.reshape(b, s, d)
```

```python
import functools
import math

import jax
import jax.numpy as jnp
from jax import lax
from jax.experimental import pallas as pl
from jax.experimental.pallas import tpu as pltpu

F32 = jnp.float32
BF16 = jnp.bfloat16
EPS = 1e-6
LANES = 128
MXU_WIDTH = 256

LRU_BLOCK_DIM = 64
LRU_C = 8.0
DIFF_HEADS = 4
DIFF_QK_DIM = 64
DIFF_ROPE_DIM = 16
MLA_HEADS = 4
MLA_NOPE_DIM = 128
MLA_ROPE_DIM = 64
MLA_QK_DIM = MLA_NOPE_DIM + MLA_ROPE_DIM
MLA_PAD_DIM = 256
MLA_V_DIM = 128
GQA_Q_HEADS = 4
GQA_KV_HEADS = 2
HEAD_DIM = 128
GRID_W = 64
ROPE_THETA = 500000.0
AXIAL_THETA = 10000.0
TOP_K = 2
LOG2E = 1.4426950408889634
VT_PAD = 16

VMEM_LIMIT = 56 * 1024 * 1024


def _params(*sem):
    return pltpu.CompilerParams(dimension_semantics=sem, vmem_limit_bytes=VMEM_LIMIT)


def _tile(dim, pref):
    return pref if dim % pref == 0 else dim


def _sigmoid(x):
    return 1.0 / (1.0 + jnp.exp(-x))


def _dot(a, b):
    return jnp.dot(a, b, preferred_element_type=F32)


def _rms(x, g):
    ms = jnp.mean(x * x, axis=-1, keepdims=True)
    return x * lax.rsqrt(ms + EPS) * g


def _norm_proj_body(x_ref, g_ref, w_ref, o_ref, xn_ref, xn_scr):
    @pl.when(pl.program_id(1) == 0)
    def _():
        xn = _rms(x_ref[...], g_ref[...]).astype(BF16)
        xn_scr[...] = xn
        xn_ref[...] = xn

    o_ref[...] = _dot(xn_scr[...], w_ref[...]).astype(o_ref.dtype)


def norm_proj(x, g, w, tm, tn):
    m, d = x.shape
    n = w.shape[1]
    return pl.pallas_call(
        _norm_proj_body,
        grid=(m // tm, n // tn),
        in_specs=[pl.BlockSpec((tm, d), lambda i, j: (i, 0)),
                  pl.BlockSpec((1, d), lambda i, j: (0, 0)),
                  pl.BlockSpec((d, tn), lambda i, j: (0, j))],
        out_specs=[pl.BlockSpec((tm, tn), lambda i, j: (i, j)),
                   pl.BlockSpec((tm, d), lambda i, j: (i, 0))],
        out_shape=[jax.ShapeDtypeStruct((m, n), BF16), jax.ShapeDtypeStruct((m, d), BF16)],
        scratch_shapes=[pltpu.VMEM((tm, d), BF16)],
        compiler_params=_params("parallel", "arbitrary"),
        name="norm_proj",
    )(x, g, w)


def _swiglu_columns(xn, wg_ref, wu_ref, o_ref):
    n = o_ref.shape[1]
    width = MXU_WIDTH if n % MXU_WIDTH == 0 else n
    for c in range(n // width):
        cols = slice(c * width, (c + 1) * width)
        gate = _dot(xn, wg_ref[:, cols])
        up = _dot(xn, wu_ref[:, cols])
        o_ref[:, cols] = (gate * _sigmoid(gate) * up).astype(o_ref.dtype)


def _norm_swiglu_body(x_ref, g_ref, wg_ref, wu_ref, o_ref, xn_scr):
    @pl.when(pl.program_id(1) == 0)
    def _():
        xn_scr[...] = _rms(x_ref[...], g_ref[...]).astype(BF16)

    _swiglu_columns(xn_scr[...], wg_ref, wu_ref, o_ref)


def norm_swiglu(x, g, w_gate_up, tm, tf):
    m, d = x.shape
    f = w_gate_up.shape[1] // 2
    nf = f // tf
    return pl.pallas_call(
        _norm_swiglu_body,
        grid=(m // tm, nf),
        in_specs=[pl.BlockSpec((tm, d), lambda i, j: (i, 0)),
                  pl.BlockSpec((1, d), lambda i, j: (0, 0)),
                  pl.BlockSpec((d, tf), lambda i, j: (0, j)),
                  pl.BlockSpec((d, tf), lambda i, j: (0, nf + j))],
        out_specs=pl.BlockSpec((tm, tf), lambda i, j: (i, j)),
        out_shape=jax.ShapeDtypeStruct((m, f), BF16),
        scratch_shapes=[pltpu.VMEM((tm, d), BF16)],
        compiler_params=_params("parallel", "arbitrary"),
        name="norm_swiglu",
    )(x, g, w_gate_up, w_gate_up)


def _mm_residual_body(a_ref, b_ref, r_ref, o_ref):
    o_ref[...] = r_ref[...] + _dot(a_ref[...], b_ref[...])


def mm_residual(a, b, res, tm, tn):
    m, k = a.shape
    n = b.shape[1]
    return pl.pallas_call(
        _mm_residual_body,
        grid=(m // tm, n // tn),
        in_specs=[pl.BlockSpec((tm, k), lambda i, j: (i, 0)),
                  pl.BlockSpec((k, tn), lambda i, j: (0, j)),
                  pl.BlockSpec((tm, tn), lambda i, j: (i, j))],
        out_specs=pl.BlockSpec((tm, tn), lambda i, j: (i, j)),
        out_shape=jax.ShapeDtypeStruct((m, n), F32),
        compiler_params=_params("parallel", "arbitrary"),
        name="mm_residual",
    )(a, b, res)


def _lru_body(ug_ref, ux_ref, cw_ref, cb_ref, gw_ref, gb_ref, lam_ref, o_ref, xpad, hf, *, chunk):
    s = ux_ref.shape[1]
    t = chunk
    nc = s // t
    halo = 8
    xpad[0:halo, :] = jnp.zeros((halo, LANES), F32)
    xpad[s + halo:s + 2 * halo, :] = jnp.zeros((halo, LANES), F32)
    xpad[halo:s + halo, :] = ux_ref[0].astype(F32)

    z = -lam_ref[...]
    softplus = jnp.maximum(z, 0.0) + jnp.log(1.0 + jnp.exp(-jnp.abs(z)))
    decay = -LRU_C * softplus
    row = lax.broadcasted_iota(jnp.int32, (t, LANES), 0)

    def coeffs(c, d):
        ext = xpad[pl.ds(pl.multiple_of(c * t, 8), t + 2 * halo), :]
        xc = cb_ref[...] + sum(cw_ref[j:j + 1, :] * ext[halo - 2 + j:halo - 2 + j + t] for j in range(4))
        xcb = xc.astype(BF16)
        r = _sigmoid(_dot(xcb, gw_ref[0, 2 * d]) + gb_ref[2 * d:2 * d + 1, :])
        i = _sigmoid(_dot(xcb, gw_ref[0, 2 * d + 1]) + gb_ref[2 * d + 1:2 * d + 2, :])
        log_a = decay[d:d + 1, :] * r
        a = jnp.exp(log_a)
        b = jnp.sqrt(1.0 - jnp.exp(2.0 * log_a)) * (i * xc)
        return a, b

    def scan(a, b, reverse):
        k = 1
        while k < t:
            if reverse:
                keep, shift = row < t - k, t - k
            else:
                keep, shift = row >= k, k
            a_s = jnp.where(keep, pltpu.roll(a, shift, 0), 1.0)
            b_s = jnp.where(keep, pltpu.roll(b, shift, 0), 0.0)
            b = a * b_s + b
            a = a * a_s
            k *= 2
        return a, b

    def fwd(c, carry):
        a, b = scan(*coeffs(c, 0), reverse=False)
        h = b + a * carry
        hf[pl.ds(pl.multiple_of(c * t, 8), t), :] = h
        return h[t - 1:t, :]

    lax.fori_loop(0, nc, fwd, jnp.zeros((1, LANES), F32))

    def bwd(i, carry):
        c = nc - 1 - i
        a, b = scan(*coeffs(c, 1), reverse=True)
        h = b + a * carry
        rows = pl.ds(pl.multiple_of(c * t, 16), t)
        g = ug_ref[0, rows, :].astype(F32)
        gelu = 0.5 * g * (1.0 + jnp.tanh(0.7978845608028654 * (g + 0.044715 * (g * g * g))))
        o_ref[0, rows, :] = ((hf[rows, :] + h) * gelu).astype(o_ref.dtype)
        return h[0:1, :]

    lax.fori_loop(0, nc, bwd, jnp.zeros((1, LANES), F32))


def lru_mixer(u, conv_w, conv_b, gate_w_bd, gate_b, lam, gate_col, x_col):
    b, s, _ = u.shape
    w = conv_w.shape[1]
    chunk = _tile(s, 256)
    return pl.pallas_call(
        functools.partial(_lru_body, chunk=chunk),
        grid=(b, w // LANES),
        in_specs=[pl.BlockSpec((1, s, LANES), lambda i, c: (i, 0, gate_col + c)),
                  pl.BlockSpec((1, s, LANES), lambda i, c: (i, 0, x_col + c)),
                  pl.BlockSpec((4, LANES), lambda i, c: (0, c)),
                  pl.BlockSpec((1, LANES), lambda i, c: (0, c)),
                  pl.BlockSpec((1, 4, LANES, LANES), lambda i, c: (c, 0, 0, 0)),
                  pl.BlockSpec((4, LANES), lambda i, c: (0, c)),
                  pl.BlockSpec((2, LANES), lambda i, c: (0, c))],
        out_specs=pl.BlockSpec((1, s, LANES), lambda i, c: (i, 0, c)),
        out_shape=jax.ShapeDtypeStruct((b, s, w), BF16),
        scratch_shapes=[pltpu.VMEM((s + 16, LANES), F32), pltpu.VMEM((s, LANES), F32)],
        compiler_params=_params("parallel", "parallel"),
        name="lru_mixer",
    )(u, u, conv_w, conv_b, gate_w_bd, gate_b, lam)


def _rope(y, c_ref, s1_ref, s2_ref, shift):
    w = y.shape[-1]
    return y * c_ref[...] + pltpu.roll(y, w - shift, 1) * s1_ref[...] + pltpu.roll(y, shift, 1) * s2_ref[...]


def _store_vt(vt_ref, h, v):
    ts, dv = v.shape
    vt_ref[0, h, 0, 0:dv, :] = v.T.astype(BF16)
    vt_ref[0, h, 0, dv:dv + VT_PAD, :] = jnp.ones((VT_PAD, ts), BF16)


def _vt_spec_shape(b, heads, s, ts, dv):
    spec = pl.BlockSpec((1, heads, 1, dv + VT_PAD, ts), lambda i, j: (i, 0, j, 0, 0))
    return spec, jax.ShapeDtypeStruct((b, heads, s // ts, dv + VT_PAD, ts), BF16)


def _diff_prep_body(q_ref, k_ref, v_ref, qg_ref, kg_ref, c_ref, s1_ref, s2_ref, qo_ref, ko_ref, vt_ref):
    lane = lax.broadcasted_iota(jnp.int32, (1, LANES), 1)
    low = lane < DIFF_QK_DIM
    scale = DIFF_QK_DIM ** -0.5 * LOG2E
    nh = DIFF_HEADS

    def norm_rope(x, g_ref):
        x2 = x * x
        ss_lo = jnp.sum(jnp.where(low, x2, 0.0), axis=-1, keepdims=True)
        ss_hi = jnp.sum(jnp.where(low, 0.0, x2), axis=-1, keepdims=True)
        ms = jnp.where(low, ss_lo, ss_hi) * (1.0 / DIFF_QK_DIM)
        y = x * lax.rsqrt(ms + EPS) * g_ref[...]
        return _rope(y, c_ref, s1_ref, s2_ref, DIFF_ROPE_DIM // 2)

    for h in range(nh):
        cols = slice(h * LANES, (h + 1) * LANES)
        q = norm_rope(q_ref[0, :, cols].astype(F32), qg_ref) * scale
        qo_ref[0, :, cols] = jnp.where(low, q, 0.0).astype(BF16)
        qo_ref[0, :, nh * LANES + h * LANES:nh * LANES + (h + 1) * LANES] = jnp.where(low, 0.0, q).astype(BF16)
        ko_ref[0, :, cols] = norm_rope(k_ref[0, :, cols].astype(F32), kg_ref).astype(BF16)
        _store_vt(vt_ref, h, v_ref[0, :, cols].astype(F32))


def diff_prep(u, q_gain, k_gain, tables, q_col, k_col, v_col, ts):
    b, s, _ = u.shape
    w = DIFF_HEADS * LANES
    tab = pl.BlockSpec((ts, LANES), lambda i, j: (j, 0))
    vec = pl.BlockSpec((1, LANES), lambda i, j: (0, 0))
    vt_spec, vt_shape = _vt_spec_shape(b, DIFF_HEADS, s, ts, LANES)
    return pl.pallas_call(
        _diff_prep_body,
        grid=(b, s // ts),
        in_specs=[pl.BlockSpec((1, ts, w), lambda i, j: (i, j, q_col)),
                  pl.BlockSpec((1, ts, w), lambda i, j: (i, j, k_col)),
                  pl.BlockSpec((1, ts, w), lambda i, j: (i, j, v_col)),
                  vec, vec, tab, tab, tab],
        out_specs=[pl.BlockSpec((1, ts, 2 * w), lambda i, j: (i, j, 0)),
                   pl.BlockSpec((1, ts, w), lambda i, j: (i, j, 0)), vt_spec],
        out_shape=[jax.ShapeDtypeStruct((b, s, 2 * w), BF16), jax.ShapeDtypeStruct((b, s, w), BF16), vt_shape],
        compiler_params=_params("parallel", "parallel"),
        name="diff_prep",
    )(u, u, u, q_gain, k_gain, *tables)


def _gqa_prep_body(q_ref, k_ref, v_ref, qg_ref, kg_ref, c_ref, s1_ref, s2_ref, qo_ref, ko_ref, vt_ref):
    scale = HEAD_DIM ** -0.5 * LOG2E
    for h in range(GQA_Q_HEADS):
        cols = slice(h * LANES, (h + 1) * LANES)
        y = _rms(q_ref[0, :, cols].astype(F32), qg_ref[...])
        qo_ref[0, :, cols] = (_rope(y, c_ref, s1_ref, s2_ref, HEAD_DIM // 4) * scale).astype(BF16)
    for h in range(GQA_KV_HEADS):
        cols = slice(h * LANES, (h + 1) * LANES)
        y = _rms(k_ref[0, :, cols].astype(F32), kg_ref[...])
        ko_ref[0, :, cols] = _rope(y, c_ref, s1_ref, s2_ref, HEAD_DIM // 4).astype(BF16)
        _store_vt(vt_ref, h, v_ref[0, :, cols].astype(F32))


def gqa_prep(u, q_gain, k_gain, tables, q_col, k_col, v_col, ts):
    b, s, _ = u.shape
    wq = GQA_Q_HEADS * HEAD_DIM
    wk = GQA_KV_HEADS * HEAD_DIM
    tab = pl.BlockSpec((ts, LANES), lambda i, j: (j, 0))
    vec = pl.BlockSpec((1, LANES), lambda i, j: (0, 0))
    vt_spec, vt_shape = _vt_spec_shape(b, GQA_KV_HEADS, s, ts, HEAD_DIM)
    return pl.pallas_call(
        _gqa_prep_body,
        grid=(b, s // ts),
        in_specs=[pl.BlockSpec((1, ts, wq), lambda i, j: (i, j, q_col)),
                  pl.BlockSpec((1, ts, wk), lambda i, j: (i, j, k_col)),
                  pl.BlockSpec((1, ts, wk), lambda i, j: (i, j, v_col)),
                  vec, vec, tab, tab, tab],
        out_specs=[pl.BlockSpec((1, ts, wq), lambda i, j: (i, j, 0)),
                   pl.BlockSpec((1, ts, wk), lambda i, j: (i, j, 0)), vt_spec],
        out_shape=[jax.ShapeDtypeStruct((b, s, wq), BF16), jax.ShapeDtypeStruct((b, s, wk), BF16), vt_shape],
        compiler_params=_params("parallel", "parallel"),
        name="gqa_prep",
    )(u, u, u, q_gain, k_gain, *tables)


def _mla_prep_body(cq_ref, ckv_ref, ckr_ref, cqg_ref, ckvg_ref, wq_ref, wk_ref, wr_ref, wv_ref,
                   qg_ref, kg_ref, c_ref, s1_ref, s2_ref, qo_ref, ko_ref, vt_ref):
    scale = MLA_QK_DIM ** -0.5 * LOG2E
    cq = _rms(cq_ref[0].astype(F32), cqg_ref[...]).astype(BF16)
    ckv = _rms(ckv_ref[0].astype(F32), ckvg_ref[...]).astype(BF16)
    q = _dot(cq, wq_ref[...])
    k = _dot(ckv, wk_ref[...]) + _dot(ckr_ref[0], wr_ref[...])
    v = _dot(ckv, wv_ref[...])

    def norm_rope(x, g_ref):
        ms = jnp.sum(x * x, axis=-1, keepdims=True) * (1.0 / MLA_QK_DIM)
        y = x * lax.rsqrt(ms + EPS) * g_ref[...]
        return _rope(y, c_ref, s1_ref, s2_ref, MLA_ROPE_DIM // 2)

    for h in range(MLA_HEADS):
        cols = slice(h * MLA_PAD_DIM, (h + 1) * MLA_PAD_DIM)
        qo_ref[0, :, cols] = (norm_rope(q[:, cols], qg_ref) * scale).astype(BF16)
        ko_ref[0, :, cols] = norm_rope(k[:, cols], kg_ref).astype(BF16)
        _store_vt(vt_ref, h, v[:, h * MLA_V_DIM:(h + 1) * MLA_V_DIM])


def mla_prep(u, cq_gain, ckv_gain, w_q, w_k, w_r, w_v, q_gain, k_gain, tables, cq_col, ckv_col, ckr_col, ts):
    b, s, _ = u.shape
    q_rank, kv_rank = w_q.shape[0], w_k.shape[0]
    wqk = MLA_HEADS * MLA_PAD_DIM
    tab = pl.BlockSpec((ts, MLA_PAD_DIM), lambda i, j: (j, 0))
    vt_spec, vt_shape = _vt_spec_shape(b, MLA_HEADS, s, ts, MLA_V_DIM)

    def full(a):
        return pl.BlockSpec(a.shape, lambda i, j: (0,) * a.ndim)

    return pl.pallas_call(
        _mla_prep_body,
        grid=(b, s // ts),
        in_specs=[pl.BlockSpec((1, ts, q_rank), lambda i, j: (i, j, cq_col)),
                  pl.BlockSpec((1, ts, kv_rank), lambda i, j: (i, j, ckv_col)),
                  pl.BlockSpec((1, ts, LANES), lambda i, j: (i, j, ckr_col)),
                  full(cq_gain), full(ckv_gain), full(w_q), full(w_k), full(w_r), full(w_v),
                  full(q_gain), full(k_gain), tab, tab, tab],
        out_specs=[pl.BlockSpec((1, ts, wqk), lambda i, j: (i, j, 0)),
                   pl.BlockSpec((1, ts, wqk), lambda i, j: (i, j, 0)), vt_spec],
        out_shape=[jax.ShapeDtypeStruct((b, s, wqk), BF16), jax.ShapeDtypeStruct((b, s, wqk), BF16), vt_shape],
        compiler_params=_params("parallel", "parallel"),
        name="mla_prep",
    )(u, u, u, cq_gain, ckv_gain, w_q, w_k, w_r, w_v, q_gain, k_gain, *tables)


def _flash_body(q_ref, k_ref, vt_ref, o_ref, sa_scr, sb_scr, acc_scr, *, tk, n_split):
    nk = k_ref.shape[1] // tk
    q = q_ref[0]
    tq = q.shape[0]
    dv = o_ref.shape[2]
    hw = tq // n_split

    def scores(c):
        k = k_ref[0, pl.ds(pl.multiple_of(c * tk, tk), tk), :]
        return lax.dot_general(k, q, (((1,), (1,)), ((), ())), preferred_element_type=F32)

    def softmax_pv(c, s_ref, m):
        m_out = []
        for h in range(n_split):
            cols = slice(h * hw, (h + 1) * hw)
            s = s_ref[:, cols]
            m_new = jnp.maximum(m[h], jnp.max(s, axis=0, keepdims=True))
            alpha = jnp.exp2(m[h] - m_new)
            p = jnp.exp2(s - m_new).astype(BF16)
            acc_scr[:, cols] = alpha * acc_scr[:, cols] + _dot(vt_ref[0, 0, c], p)
            m_out.append(m_new)
        return tuple(m_out)

    def pair(j, m):
        c = 2 * j
        sb_scr[...] = scores(c + 1)
        m = softmax_pv(c, sa_scr, m)
        sa_scr[...] = scores(c + 2)
        return softmax_pv(c + 1, sb_scr, m)

    acc_scr[...] = jnp.zeros(acc_scr.shape, F32)
    sa_scr[...] = scores(0)
    m = tuple(jnp.full((1, hw), -jnp.inf, F32) for _ in range(n_split))
    m = lax.fori_loop(0, nk // 2 - 1, pair, m)
    sb_scr[...] = scores(nk - 1)
    m = softmax_pv(nk - 2, sa_scr, m)
    softmax_pv(nk - 1, sb_scr, m)
    acc = acc_scr[...]
    o_ref[0] = (acc[0:dv, :] / acc[dv:dv + 1, :]).T.astype(o_ref.dtype)


def flash_attention(q, k, vt, n_maps, dk, q_col, k_col, v_head, out_dtype, tq):
    b, s, _ = q.shape
    _, _, nk, dva, tk = vt.shape
    dv = dva - VT_PAD
    return pl.pallas_call(
        functools.partial(_flash_body, tk=tk, n_split=max(1, tq // MXU_WIDTH)),
        grid=(b, n_maps, s // tq),
        in_specs=[pl.BlockSpec((1, tq, dk), lambda i, h, a: (i, a, q_col(h))),
                  pl.BlockSpec((1, s, dk), lambda i, h, a: (i, 0, k_col(h))),
                  pl.BlockSpec((1, 1, nk, dva, tk), lambda i, h, a: (i, v_head(h), 0, 0, 0))],
        out_specs=pl.BlockSpec((1, tq, dv), lambda i, h, a: (i, a, h)),
        out_shape=jax.ShapeDtypeStruct((b, s, n_maps * dv), out_dtype),
        scratch_shapes=[pltpu.VMEM((tk, tq), F32), pltpu.VMEM((tk, tq), F32), pltpu.VMEM((dva, tq), F32)],
        compiler_params=_params("parallel", "parallel", "arbitrary"),
        name="flash_attention",
    )(q, k, vt)


def _diff_combine_body(o_ref, lam_ref, g_ref, y_ref, *, lambda_init):
    lam = lam_ref[...]
    lam_full = (jnp.exp(jnp.sum(lam[0:1] * lam[1:2], keepdims=True))
                - jnp.exp(jnp.sum(lam[2:3] * lam[3:4], keepdims=True)) + lambda_init)
    w = DIFF_HEADS * LANES
    for h in range(DIFF_HEADS):
        cols = slice(h * LANES, (h + 1) * LANES)
        d = o_ref[0, :, cols] - lam_full * o_ref[0, :, w + h * LANES:w + (h + 1) * LANES]
        y_ref[0, :, cols] = (_rms(d, g_ref[...]) * (1.0 - lambda_init)).astype(y_ref.dtype)


def diff_combine(o, lam, out_gain, lambda_init, ts):
    b, s, w2 = o.shape
    w = w2 // 2
    return pl.pallas_call(
        functools.partial(_diff_combine_body, lambda_init=lambda_init),
        grid=(b, s // ts),
        in_specs=[pl.BlockSpec((1, ts, w2), lambda i, j: (i, j, 0)),
                  pl.BlockSpec(lam.shape, lambda i, j: (0, 0)),
                  pl.BlockSpec((1, LANES), lambda i, j: (0, 0))],
        out_specs=pl.BlockSpec((1, ts, w), lambda i, j: (i, j, 0)),
        out_shape=jax.ShapeDtypeStruct((b, s, w), BF16),
        compiler_params=_params("parallel", "parallel"),
        name="diff_combine",
    )(o, lam, out_gain)


def _merge_body(xn_ref, ya_ref, yb_ref, yc_ref, yd_ref, wg0, wg1, wg2, wg3, p0, p1, p2, p3, o_ref):
    xn = xn_ref[...]
    acc = None
    for y_ref, wg_ref, p_ref in ((ya_ref, wg0, p0), (yb_ref, wg1, p1), (yc_ref, wg2, p2), (yd_ref, wg3, p3)):
        term = _sigmoid(_dot(xn, wg_ref[...])) * _dot(y_ref[...], p_ref[0])
        acc = term if acc is None else acc + term
    o_ref[...] = acc.astype(o_ref.dtype)


def gated_merge(xn, ys, w_gate, w_branch, tm, tn):
    m, d = xn.shape
    width = w_branch.shape[1]
    nj = d // tn
    y_spec = pl.BlockSpec((tm, width), lambda i, j: (i, 0))
    wg_specs = [pl.BlockSpec((d, tn), functools.partial(lambda i, j, br: (0, br * nj + j), br=br)) for br in range(4)]
    p_specs = [pl.BlockSpec((1, width, tn), functools.partial(lambda i, j, br: (br, 0, j), br=br)) for br in range(4)]
    return pl.pallas_call(
        _merge_body,
        grid=(m // tm, nj),
        in_specs=[pl.BlockSpec((tm, d), lambda i, j: (i, 0)), y_spec, y_spec, y_spec, y_spec] + wg_specs + p_specs,
        out_specs=pl.BlockSpec((tm, tn), lambda i, j: (i, j)),
        out_shape=jax.ShapeDtypeStruct((m, d), BF16),
        compiler_params=_params("parallel", "arbitrary"),
        name="gated_merge",
    )(xn, *ys, w_gate, w_gate, w_gate, w_gate, w_branch, w_branch, w_branch, w_branch)


def _router_body(x_ref, g_ref, r_ref, idx_ref, w_ref, *, n_experts):
    xn = _rms(x_ref[...], g_ref[...])
    logits = jnp.dot(xn, r_ref[...], precision=lax.Precision.HIGHEST, preferred_element_type=F32)
    lane = lax.broadcasted_iota(jnp.int32, logits.shape, 1).astype(F32)
    neg = -jnp.inf
    lg = jnp.where(lane < n_experts, logits, neg)
    v1 = jnp.max(lg, axis=-1, keepdims=True)
    i1 = jnp.min(jnp.where(lg == v1, lane, float(LANES)), axis=-1, keepdims=True)
    lg2 = jnp.where(lane == i1, neg, lg)
    v2 = jnp.max(lg2, axis=-1, keepdims=True)
    i2 = jnp.min(jnp.where(lg2 == v2, lane, float(LANES)), axis=-1, keepdims=True)
    e21 = jnp.exp(v2 - v1)
    w1 = 1.0 / (1.0 + e21)
    w2 = e21 * w1
    idx_ref[...] = jnp.where(lane == 0.0, i1, jnp.where(lane == 1.0, i2, 0.0)).astype(jnp.int32)
    w_ref[...] = jnp.where(lane == 0.0, w1, jnp.where(lane == 1.0, w2, 0.0))


def moe_router(x, g, router_padded, n_experts, tm):
    m, d = x.shape
    return pl.pallas_call(
        functools.partial(_router_body, n_experts=n_experts),
        grid=(m // tm,),
        in_specs=[pl.BlockSpec((tm, d), lambda i: (i, 0)),
                  pl.BlockSpec((1, d), lambda i: (0, 0)),
                  pl.BlockSpec((d, LANES), lambda i: (0, 0))],
        out_specs=[pl.BlockSpec((tm, LANES), lambda i: (i, 0)), pl.BlockSpec((tm, LANES), lambda i: (i, 0))],
        out_shape=[jax.ShapeDtypeStruct((m, LANES), jnp.int32), jax.ShapeDtypeStruct((m, LANES), F32)],
        compiler_params=_params("parallel"),
        name="moe_router",
    )(x, g, router_padded)


def _moe_up_body(te_ref, nv_ref, tok_ref, x_hbm, g_ref, wg_ref, wu_ref, o_ref, xbuf, xn_scr, sem, *, rows_per_step):
    t = pl.program_id(0)
    f = pl.program_id(1)
    tm = xn_scr.shape[0]
    nt = pl.num_programs(0)
    nf = pl.num_programs(1)
    n_rows = rows_per_step * nf
    slot = t % 2

    def row_copy(dst_slot, r, tok):
        return pltpu.make_async_copy(x_hbm.at[pl.ds(tok, 1)], xbuf.at[dst_slot, pl.ds(r, 1)], sem.at[dst_slot])

    def prefetch_next_tile():
        for i in range(rows_per_step):
            r = f * rows_per_step + i
            row_copy(1 - slot, r, tok_ref[(t + 1) * tm + r]).start()

    def wait_rows(dst_slot):
        def wait(r, c):
            row_copy(dst_slot, r, 0).wait()
            return c
        lax.fori_loop(0, n_rows, wait, 0)

    @pl.when(jnp.logical_and(t == 0, f == 0))
    def _():
        def start(r, c):
            row_copy(0, r, tok_ref[r]).start()
            return c
        lax.fori_loop(0, n_rows, start, 0)

    @pl.when(f == 0)
    def _():
        wait_rows(slot)

    @pl.when(t < nv_ref[0])
    def _():
        @pl.when(f == 0)
        def _():
            xn_scr[...] = _rms(xbuf[slot, 0:tm, :], g_ref[...]).astype(BF16)

        prefetch_next_tile()
        _swiglu_columns(xn_scr[...], wg_ref.at[0], wu_ref.at[0], o_ref)

    @pl.when(t >= nv_ref[0])
    def _():
        prefetch_next_tile()
        o_ref[...] = jnp.zeros(o_ref.shape, o_ref.dtype)

    @pl.when(jnp.logical_and(t == nt - 1, f == nf - 1))
    def _():
        wait_rows(1 - slot)


def moe_up(tile_expert, n_valid, row_token, x, g, w_gate_up, tm, tf):
    d = x.shape[1]
    f = w_gate_up.shape[2] // 2
    nf = f // tf
    p = row_token.shape[0] - 2 * tm
    rows_per_step = -(-tm // nf)
    buf_rows = -(-rows_per_step * nf // 8) * 8

    def wmap(off, t, j, te, nv, tok):
        return (te[t], 0, off + jnp.where(t < nv[0], j, nf - 1))

    grid_spec = pltpu.PrefetchScalarGridSpec(
        num_scalar_prefetch=3,
        grid=(p // tm, nf),
        in_specs=[pl.BlockSpec(memory_space=pl.ANY),
                  pl.BlockSpec((1, d), lambda t, j, te, nv, tok: (0, 0)),
                  pl.BlockSpec((1, d, tf), functools.partial(wmap, 0)),
                  pl.BlockSpec((1, d, tf), functools.partial(wmap, nf))],
        out_specs=pl.BlockSpec((tm, tf), lambda t, j, te, nv, tok: (t, j)),
        scratch_shapes=[pltpu.VMEM((2, buf_rows, d), F32), pltpu.VMEM((tm, d), BF16), pltpu.SemaphoreType.DMA((2,))],
    )
    return pl.pallas_call(
        functools.partial(_moe_up_body, rows_per_step=rows_per_step),
        grid_spec=grid_spec,
        out_shape=jax.ShapeDtypeStruct((p, f), BF16),
        compiler_params=_params("arbitrary", "arbitrary"),
        name="moe_up",
    )(tile_expert, n_valid, row_token, x, g, w_gate_up, w_gate_up)


def _moe_down_body(te_ref, nv_ref, h_ref, w_ref, o_ref):
    t = pl.program_id(0)

    @pl.when(t < nv_ref[0])
    def _():
        o_ref[...] = _dot(h_ref[...], w_ref[0])

    @pl.when(t >= nv_ref[0])
    def _():
        o_ref[...] = jnp.zeros(o_ref.shape, o_ref.dtype)


def moe_down(tile_expert, n_valid, h, w_down, tm, tn):
    p, f = h.shape
    d = w_down.shape[2]
    nj = d // tn
    grid_spec = pltpu.PrefetchScalarGridSpec(
        num_scalar_prefetch=2,
        grid=(p // tm, nj),
        in_specs=[pl.BlockSpec((tm, f), lambda t, j, te, nv: (t, 0)),
                  pl.BlockSpec((1, f, tn), lambda t, j, te, nv: (te[t], 0, jnp.where(t < nv[0], j, nj - 1)))],
        out_specs=pl.BlockSpec((tm, tn), lambda t, j, te, nv: (t, j)),
    )
    return pl.pallas_call(
        _moe_down_body,
        grid_spec=grid_spec,
        out_shape=jax.ShapeDtypeStruct((p, d), F32),
        compiler_params=_params("arbitrary", "arbitrary"),
        name="moe_down",
    )(tile_expert, n_valid, h, w_down)


def _moe_combine_body(pos_ref, x_ref, w_ref, y_hbm, o_ref, buf0, buf1, sem):
    i = pl.program_id(0)
    tc = buf0.shape[0]

    def row_copy(r, k, src):
        dst = buf0 if k == 0 else buf1
        return pltpu.make_async_copy(y_hbm.at[pl.ds(src, 1)], dst.at[pl.ds(r, 1)], sem)

    def start(r, c):
        for k in range(TOP_K):
            row_copy(r, k, pos_ref[(i * tc + r) * TOP_K + k]).start()
        return c

    def wait(r, c):
        for k in range(TOP_K):
            row_copy(r, k, 0).wait()
        return c

    lax.fori_loop(0, tc, start, 0, unroll=8)
    lax.fori_loop(0, tc, wait, 0, unroll=8)
    o_ref[...] = x_ref[...] + w_ref[:, 0:1] * buf0[...] + w_ref[:, 1:2] * buf1[...]


def moe_combine(pos, x, wts, y, tc):
    m, d = x.shape
    grid_spec = pltpu.PrefetchScalarGridSpec(
        num_scalar_prefetch=1,
        grid=(m // tc,),
        in_specs=[pl.BlockSpec((tc, d), lambda i, pos: (i, 0)),
                  pl.BlockSpec((tc, LANES), lambda i, pos: (i, 0)),
                  pl.BlockSpec(memory_space=pl.ANY)],
        out_specs=pl.BlockSpec((tc, d), lambda i, pos: (i, 0)),
        scratch_shapes=[pltpu.VMEM((tc, d), F32), pltpu.VMEM((tc, d), F32), pltpu.SemaphoreType.DMA(())],
    )
    return pl.pallas_call(
        _moe_combine_body,
        grid_spec=grid_spec,
        out_shape=jax.ShapeDtypeStruct((m, d), F32),
        compiler_params=_params("arbitrary"),
        name="moe_combine",
    )(pos, x, wts, y)


def _moe_schedule(idx, n_experts, tm):
    m = idx.shape[0]
    n_assign = m * TOP_K
    e_flat = idx.reshape(n_assign)
    onehot = (e_flat[:, None] == jnp.arange(n_experts, dtype=jnp.int32)[None, :]).astype(jnp.int32)
    csum = jnp.cumsum(onehot, axis=0)
    rank = jnp.take_along_axis(csum, e_flat[:, None], axis=1)[:, 0] - 1
    counts = csum[-1]
    padded = ((counts + tm - 1) // tm) * tm
    ends = jnp.cumsum(padded)
    pos = (ends - padded)[e_flat] + rank
    n_rows = n_assign + n_experts * tm
    row_token = jnp.zeros((n_rows + 2 * tm,), jnp.int32).at[pos].set(jnp.arange(n_assign, dtype=jnp.int32) // TOP_K)
    n_tiles = n_rows // tm
    tiles = jnp.arange(n_tiles, dtype=jnp.int32)
    n_valid = (ends[-1] // tm).astype(jnp.int32)
    tile_expert = jnp.sum((tiles[:, None] * tm >= ends[None, :]).astype(jnp.int32), axis=1)
    tile_expert = jnp.minimum(tile_expert, n_experts - 1)
    tile_expert = jnp.where(tiles < n_valid, tile_expert, tile_expert[n_valid - 1])
    return pos.astype(jnp.int32), row_token, tile_expert, n_valid.reshape(1)


def moe_ffn(x, g, router, w_gate_up, w_down):
    m, d = x.shape
    n_experts = router.shape[1]
    tm = _tile(m, 512)
    router_padded = jnp.pad(router, ((0, 0), (0, LANES - n_experts)))
    idx, wts = moe_router(x, g, router_padded, n_experts, _tile(m, 512))
    pos, row_token, tile_expert, n_valid = _moe_schedule(idx[:, :TOP_K], n_experts, tm)
    f = w_down.shape[1]
    tf = next((c for c in (7 * MXU_WIDTH, 4 * MXU_WIDTH, 2 * MXU_WIDTH) if f % c == 0), f)
    h = moe_up(tile_expert, n_valid, row_token, x, g, w_gate_up, tm, tf)
    y = moe_down(tile_expert, n_valid, h, w_down, tm, _tile(d, 512))
    return moe_combine(pos, x, wts, y, _tile(m, 256))


def _rope_cos_sin(pos, dim, theta):
    inv = jnp.power(jnp.float32(theta), -jnp.arange(0, dim, 2, dtype=F32) / dim)
    ang = pos.astype(F32)[:, None] * inv[None, :]
    return jnp.cos(ang), jnp.sin(ang)


def _rope_tables(s):
    pos = jnp.arange(s, dtype=jnp.int32)
    rows = s // GRID_W
    row_pos = jnp.broadcast_to(jnp.arange(rows, dtype=jnp.int32)[:, None], (rows, GRID_W)).reshape(-1)
    col_pos = jnp.broadcast_to(jnp.arange(GRID_W, dtype=jnp.int32)[None, :], (rows, GRID_W)).reshape(-1)

    def ones(n):
        return jnp.ones((s, n), F32)

    def zeros(n):
        return jnp.zeros((s, n), F32)

    cat = functools.partial(jnp.concatenate, axis=1)
    cos, sin = _rope_cos_sin(pos, DIFF_ROPE_DIM, ROPE_THETA)
    rest = DIFF_QK_DIM - DIFF_ROPE_DIM
    half = DIFF_ROPE_DIM // 2
    diff = (cat([cos, cos, ones(rest)] * 2), cat([-sin, zeros(half), zeros(rest)] * 2),
            cat([zeros(half), sin, zeros(rest)] * 2))
    cos, sin = _rope_cos_sin(pos, MLA_ROPE_DIM, ROPE_THETA)
    half = MLA_ROPE_DIM // 2
    tail = MLA_PAD_DIM - MLA_QK_DIM
    mla = (cat([ones(MLA_NOPE_DIM), cos, cos, ones(tail)]),
           cat([zeros(MLA_NOPE_DIM), -sin, zeros(half), zeros(tail)]),
           cat([zeros(MLA_NOPE_DIM), zeros(half), sin, zeros(tail)]))
    cos_r, sin_r = _rope_cos_sin(row_pos, HEAD_DIM // 2, AXIAL_THETA)
    cos_c, sin_c = _rope_cos_sin(col_pos, HEAD_DIM // 2, AXIAL_THETA)
    half = HEAD_DIM // 4
    gqa = (cat([cos_r, cos_r, cos_c, cos_c]), cat([-sin_r, zeros(half), -sin_c, zeros(half)]),
           cat([zeros(half), sin_r, zeros(half), sin_c]))
    return diff, mla, gqa


COL_A_GATE, COL_A_X, COL_B_Q, COL_B_K, COL_B_V, COL_C_Q, COL_D_Q = (i * 512 for i in range(7))
COL_C_KV, COL_D_K, COL_D_V, COL_C_KR = 3584, 3840, 4096, 4352
MIX_COLS = 4608
MIX_TN = 3 * MXU_WIDTH


def _mixer_in_weights(w_in_l):
    widths = (512, 512, 512, 512, 512, 512, 256, 64, 512, 256, 256)
    offs = [0]
    for wd in widths:
        offs.append(offs[-1] + wd)
    (a_gate, a_x, b_q, b_k, b_v, c_q, c_kv, c_kr, d_q, d_k, d_v) = (
        w_in_l[:, offs[i]:offs[i + 1]] for i in range(len(widths)))
    pad = jnp.zeros((w_in_l.shape[0], MIX_COLS - COL_C_KR - MLA_ROPE_DIM), w_in_l.dtype)
    w_mix = jnp.concatenate([a_gate, a_x, b_q, b_k, b_v, c_q, d_q, c_kv, d_k, d_v, c_kr, pad], axis=1)
    return w_mix.astype(BF16), w_in_l[:, offs[-1]:].astype(BF16)


def _mla_weights(w_uq, w_ukv):
    q_rank, kv_rank = w_uq.shape[0], w_ukv.shape[0]
    wq = w_uq.reshape(q_rank, MLA_HEADS, MLA_QK_DIM)
    wq = jnp.pad(wq, ((0, 0), (0, 0), (0, MLA_PAD_DIM - MLA_QK_DIM))).reshape(q_rank, MLA_HEADS * MLA_PAD_DIM)
    wkv = w_ukv.reshape(kv_rank, MLA_HEADS, MLA_NOPE_DIM + MLA_V_DIM)
    wk = jnp.pad(wkv[:, :, :MLA_NOPE_DIM], ((0, 0), (0, 0), (0, MLA_PAD_DIM - MLA_NOPE_DIM)))
    wk = wk.reshape(kv_rank, MLA_HEADS * MLA_PAD_DIM)
    wv = wkv[:, :, MLA_NOPE_DIM:].reshape(kv_rank, MLA_HEADS * MLA_V_DIM)
    place = jnp.zeros((LANES, MLA_HEADS, MLA_PAD_DIM), F32)
    r = jnp.arange(MLA_ROPE_DIM)
    place = place.at[r, :, MLA_NOPE_DIM + r].set(1.0).reshape(LANES, MLA_HEADS * MLA_PAD_DIM)
    return wq.astype(BF16), wk.astype(BF16), place.astype(BF16), wv.astype(BF16)


def _lru_gate_weights(gate_w):
    nb = gate_w.shape[2]
    per = LANES // LRU_BLOCK_DIM
    w = gate_w.reshape(4, nb // per, per, LRU_BLOCK_DIM, LRU_BLOCK_DIM)
    bd = jnp.einsum("dcpkj,pq->cdpkqj", w, jnp.eye(per, dtype=w.dtype))
    return bd.reshape(nb // per, 4, LANES, LANES).astype(BF16)


def _pad_gain(g, width):
    return jnp.pad(g, (0, width - g.shape[0])).reshape(1, width)


def kernel(x, norm_mix, w_in, lru_conv_w, lru_conv_b, lru_gate_w, lru_gate_b, lru_lambda, diff_q_norm, diff_k_norm, diff_lambda, diff_out_norm, mla_cq_norm, mla_ckv_norm, mla_w_uq, mla_w_ukv, mla_q_norm, mla_k_norm, gqa_q_norm, gqa_k_norm, w_branch, w_out, norm_ffn, ffn_w_gate_up, ffn_w_down, moe_router, moe_w_gate_up, moe_w_down):
    b, s, d = x.shape
    m = b * s
    depth = w_in.shape[0]
    lru_width = lru_conv_w.shape[2]
    tables_diff, tables_mla, tables_gqa = _rope_tables(s)
    ts = 512 if s % 1024 == 0 else s // 2
    tq = _tile(s, 1024)
    tm = _tile(m, 512)
    tm_big = _tile(m, 1024)
    xf = x.reshape(m, d)

    for l in range(depth):
        w_mix, w_gate = _mixer_in_weights(w_in[l])
        u, xn = norm_proj(xf, norm_mix[l].reshape(1, d), w_mix, tm_big, MIX_TN)
        u = u.reshape(b, s, MIX_COLS)

        y_a = lru_mixer(u, lru_conv_w[l], lru_conv_b[l].reshape(1, lru_width), _lru_gate_weights(lru_gate_w[l]),
                        lru_gate_b[l].reshape(4, lru_width), lru_lambda[l], COL_A_GATE // LANES, COL_A_X // LANES)

        lambda_init = 0.8 - 0.6 * math.exp(-0.3 * l)
        q_b, k_b, vt_b = diff_prep(u, jnp.tile(diff_q_norm[l], 2).reshape(1, LANES),
                                   jnp.tile(diff_k_norm[l], 2).reshape(1, LANES),
                                   tables_diff, COL_B_Q // 512, COL_B_K // 512, COL_B_V // 512, ts)
        o_b = flash_attention(q_b, k_b, vt_b, 2 * DIFF_HEADS, LANES,
                              lambda h: h, lambda h: h % DIFF_HEADS, lambda h: h % DIFF_HEADS, F32, tq)
        y_b = diff_combine(o_b, diff_lambda[l], diff_out_norm[l].reshape(1, LANES), lambda_init, ts)

        wq, wk, wr, wv = _mla_weights(mla_w_uq[l], mla_w_ukv[l])
        q_c, k_c, vt_c = mla_prep(u, mla_cq_norm[l].reshape(1, -1), mla_ckv_norm[l].reshape(1, -1), wq, wk, wr, wv,
                                 _pad_gain(mla_q_norm[l], MLA_PAD_DIM), _pad_gain(mla_k_norm[l], MLA_PAD_DIM),
                                 tables_mla, COL_C_Q // 512, COL_C_KV // 256, COL_C_KR // LANES, ts)
        y_c = flash_attention(q_c, k_c, vt_c, MLA_HEADS, MLA_PAD_DIM,
                              lambda h: h, lambda h: h, lambda h: h, BF16, tq)

        group = GQA_Q_HEADS // GQA_KV_HEADS
        q_d, k_d, vt_d = gqa_prep(u, gqa_q_norm[l].reshape(1, LANES), gqa_k_norm[l].reshape(1, LANES), tables_gqa,
                                  COL_D_Q // 512, COL_D_K // 256, COL_D_V // 256, ts)
        y_d = flash_attention(q_d, k_d, vt_d, GQA_Q_HEADS, HEAD_DIM,
                              lambda h: h, lambda h: h // group, lambda h: h // group, BF16, tq)

        ys = [y.reshape(m, -1) for y in (y_a, y_b, y_c, y_d)]
        merged = gated_merge(xn, ys, w_gate, w_branch[l].astype(BF16), tm, _tile(d, 512))
        xf = mm_residual(merged, w_out[l].astype(BF16), xf, tm_big, _tile(d, 512))

        g_ffn = norm_ffn[l].reshape(1, d)
        if l % 2 == 0:
            f = ffn_w_down.shape[1]
            act = norm_swiglu(xf, g_ffn, ffn_w_gate_up[l // 2].astype(BF16), tm_big, _tile(f, 512))
            xf = mm_residual(act, ffn_w_down[l // 2].astype(BF16), xf, tm_big, _tile(d, 512))
        else:
            xf = moe_ffn(xf, g_ffn, moe_router[l // 2], moe_w_gate_up[l // 2].astype(BF16),
                         moe_w_down[l // 2].astype(BF16))
    return xf.reshape(b, s, d)
```

```python
import functools
import math

import jax
import jax.numpy as jnp
from jax import lax
from jax.experimental import pallas as pl
from jax.experimental.pallas import tpu as pltpu

F32 = jnp.float32
BF16 = jnp.bfloat16
EPS = 1e-6
LANES = 128
MXU_WIDTH = 256

LRU_BLOCK_DIM = 64
LRU_C = 8.0
DIFF_HEADS = 4
DIFF_QK_DIM = 64
DIFF_ROPE_DIM = 16
MLA_HEADS = 4
MLA_NOPE_DIM = 128
MLA_ROPE_DIM = 64
MLA_QK_DIM = MLA_NOPE_DIM + MLA_ROPE_DIM
MLA_PAD_DIM = 256
MLA_V_DIM = 128
GQA_Q_HEADS = 4
GQA_KV_HEADS = 2
HEAD_DIM = 128
GRID_W = 64
ROPE_THETA = 500000.0
AXIAL_THETA = 10000.0
TOP_K = 2
LOG2E = 1.4426950408889634
VT_PAD = 16

VMEM_LIMIT = 56 * 1024 * 1024


def _params(*sem):
    return pltpu.CompilerParams(dimension_semantics=sem, vmem_limit_bytes=VMEM_LIMIT)


def _tile(dim, pref):
    return pref if dim % pref == 0 else dim


def _sigmoid(x):
    return 1.0 / (1.0 + jnp.exp(-x))


def _dot(a, b):
    return jnp.dot(a, b, preferred_element_type=F32)


def _rms(x, g):
    ms = jnp.mean(x * x, axis=-1, keepdims=True)
    return x * lax.rsqrt(ms + EPS) * g


def _norm_proj_body(x_ref, g_ref, w_ref, o_ref, xn_ref, xn_scr):
    @pl.when(pl.program_id(1) == 0)
    def _():
        xn = _rms(x_ref[...], g_ref[...]).astype(BF16)
        xn_scr[...] = xn
        xn_ref[...] = xn

    o_ref[...] = _dot(xn_scr[...], w_ref[...]).astype(o_ref.dtype)


def norm_proj(x, g, w, tm, tn):
    m, d = x.shape
    n = w.shape[1]
    return pl.pallas_call(
        _norm_proj_body,
        grid=(m // tm, n // tn),
        in_specs=[pl.BlockSpec((tm, d), lambda i, j: (i, 0)),
                  pl.BlockSpec((1, d), lambda i, j: (0, 0)),
                  pl.BlockSpec((d, tn), lambda i, j: (0, j))],
        out_specs=[pl.BlockSpec((tm, tn), lambda i, j: (i, j)),
                   pl.BlockSpec((tm, d), lambda i, j: (i, 0))],
        out_shape=[jax.ShapeDtypeStruct((m, n), BF16), jax.ShapeDtypeStruct((m, d), BF16)],
        scratch_shapes=[pltpu.VMEM((tm, d), BF16)],
        compiler_params=_params("parallel", "arbitrary"),
        name="norm_proj",
    )(x, g, w)


def _swiglu_columns(xn, wg_ref, wu_ref, o_ref):
    n = o_ref.shape[1]
    width = MXU_WIDTH if n % MXU_WIDTH == 0 else n
    for c in range(n // width):
        cols = slice(c * width, (c + 1) * width)
        gate = _dot(xn, wg_ref[:, cols])
        up = _dot(xn, wu_ref[:, cols])
        o_ref[:, cols] = (gate * _sigmoid(gate) * up).astype(o_ref.dtype)


def _norm_swiglu_body(x_ref, g_ref, wg_ref, wu_ref, o_ref, xn_scr):
    @pl.when(pl.program_id(1) == 0)
    def _():
        xn_scr[...] = _rms(x_ref[...], g_ref[...]).astype(BF16)

    _swiglu_columns(xn_scr[...], wg_ref, wu_ref, o_ref)


def norm_swiglu(x, g, w_gate_up, tm, tf):
    m, d = x.shape
    f = w_gate_up.shape[1] // 2
    nf = f // tf
    return pl.pallas_call(
        _norm_swiglu_body,
        grid=(m // tm, nf),
        in_specs=[pl.BlockSpec((tm, d), lambda i, j: (i, 0)),
                  pl.BlockSpec((1, d), lambda i, j: (0, 0)),
                  pl.BlockSpec((d, tf), lambda i, j: (0, j)),
                  pl.BlockSpec((d, tf), lambda i, j: (0, nf + j))],
        out_specs=pl.BlockSpec((tm, tf), lambda i, j: (i, j)),
        out_shape=jax.ShapeDtypeStruct((m, f), BF16),
        scratch_shapes=[pltpu.VMEM((tm, d), BF16)],
        compiler_params=_params("parallel", "arbitrary"),
        name="norm_swiglu",
    )(x, g, w_gate_up, w_gate_up)


def _mm_residual_body(a_ref, b_ref, r_ref, o_ref):
    o_ref[...] = r_ref[...] + _dot(a_ref[...], b_ref[...])


def mm_residual(a, b, res, tm, tn):
    m, k = a.shape
    n = b.shape[1]
    return pl.pallas_call(
        _mm_residual_body,
        grid=(m // tm, n // tn),
        in_specs=[pl.BlockSpec((tm, k), lambda i, j: (i, 0)),
                  pl.BlockSpec((k, tn), lambda i, j: (0, j)),
                  pl.BlockSpec((tm, tn), lambda i, j: (i, j))],
        out_specs=pl.BlockSpec((tm, tn), lambda i, j: (i, j)),
        out_shape=jax.ShapeDtypeStruct((m, n), F32),
        compiler_params=_params("parallel", "arbitrary"),
        name="mm_residual",
    )(a, b, res)


def _lru_body(ug_ref, ux_ref, cw_ref, cb_ref, gw_ref, gb_ref, lam_ref, o_ref, xpad, hf, *, chunk):
    s = ux_ref.shape[1]
    t = chunk
    nc = s // t
    halo = 8
    xpad[0:halo, :] = jnp.zeros((halo, LANES), F32)
    xpad[s + halo:s + 2 * halo, :] = jnp.zeros((halo, LANES), F32)
    xpad[halo:s + halo, :] = ux_ref[0].astype(F32)

    z = -lam_ref[...]
    softplus = jnp.maximum(z, 0.0) + jnp.log(1.0 + jnp.exp(-jnp.abs(z)))
    decay = -LRU_C * softplus
    row = lax.broadcasted_iota(jnp.int32, (t, LANES), 0)

    def coeffs(c, d):
        ext = xpad[pl.ds(pl.multiple_of(c * t, 8), t + 2 * halo), :]
        xc = cb_ref[...] + sum(cw_ref[j:j + 1, :] * ext[halo - 2 + j:halo - 2 + j + t] for j in range(4))
        xcb = xc.astype(BF16)
        r = _sigmoid(_dot(xcb, gw_ref[0, 2 * d]) + gb_ref[2 * d:2 * d + 1, :])
        i = _sigmoid(_dot(xcb, gw_ref[0, 2 * d + 1]) + gb_ref[2 * d + 1:2 * d + 2, :])
        log_a = decay[d:d + 1, :] * r
        a = jnp.exp(log_a)
        b = jnp.sqrt(1.0 - jnp.exp(2.0 * log_a)) * (i * xc)
        return a, b

    def scan(a, b, reverse):
        k = 1
        while k < t:
            if reverse:
                keep, shift = row < t - k, t - k
            else:
                keep, shift = row >= k, k
            a_s = jnp.where(keep, pltpu.roll(a, shift, 0), 1.0)
            b_s = jnp.where(keep, pltpu.roll(b, shift, 0), 0.0)
            b = a * b_s + b
            a = a * a_s
            k *= 2
        return a, b

    def fwd(c, carry):
        a, b = scan(*coeffs(c, 0), reverse=False)
        h = b + a * carry
        hf[pl.ds(pl.multiple_of(c * t, 8), t), :] = h
        return h[t - 1:t, :]

    lax.fori_loop(0, nc, fwd, jnp.zeros((1, LANES), F32))

    def bwd(i, carry):
        c = nc - 1 - i
        a, b = scan(*coeffs(c, 1), reverse=True)
        h = b + a * carry
        rows = pl.ds(pl.multiple_of(c * t, 16), t)
        g = ug_ref[0, rows, :].astype(F32)
        gelu = 0.5 * g * (1.0 + jnp.tanh(0.7978845608028654 * (g + 0.044715 * (g * g * g))))
        o_ref[0, rows, :] = ((hf[rows, :] + h) * gelu).astype(o_ref.dtype)
        return h[0:1, :]

    lax.fori_loop(0, nc, bwd, jnp.zeros((1, LANES), F32))


def lru_mixer(u, conv_w, conv_b, gate_w_bd, gate_b, lam, gate_col, x_col):
    b, s, _ = u.shape
    w = conv_w.shape[1]
    chunk = _tile(s, 256)
    return pl.pallas_call(
        functools.partial(_lru_body, chunk=chunk),
        grid=(b, w // LANES),
        in_specs=[pl.BlockSpec((1, s, LANES), lambda i, c: (i, 0, gate_col + c)),
                  pl.BlockSpec((1, s, LANES), lambda i, c: (i, 0, x_col + c)),
                  pl.BlockSpec((4, LANES), lambda i, c: (0, c)),
                  pl.BlockSpec((1, LANES), lambda i, c: (0, c)),
                  pl.BlockSpec((1, 4, LANES, LANES), lambda i, c: (c, 0, 0, 0)),
                  pl.BlockSpec((4, LANES), lambda i, c: (0, c)),
                  pl.BlockSpec((2, LANES), lambda i, c: (0, c))],
        out_specs=pl.BlockSpec((1, s, LANES), lambda i, c: (i, 0, c)),
        out_shape=jax.ShapeDtypeStruct((b, s, w), BF16),
        scratch_shapes=[pltpu.VMEM((s + 16, LANES), F32), pltpu.VMEM((s, LANES), F32)],
        compiler_params=_params("parallel", "parallel"),
        name="lru_mixer",
    )(u, u, conv_w, conv_b, gate_w_bd, gate_b, lam)


def _rope(y, c_ref, s1_ref, s2_ref, shift):
    w = y.shape[-1]
    return y * c_ref[...] + pltpu.roll(y, w - shift, 1) * s1_ref[...] + pltpu.roll(y, shift, 1) * s2_ref[...]


def _store_vt(vt_ref, h, v):
    ts, dv = v.shape
    vt_ref[0, h, 0, 0:dv, :] = v.T.astype(BF16)
    vt_ref[0, h, 0, dv:dv + VT_PAD, :] = jnp.ones((VT_PAD, ts), BF16)


def _vt_spec_shape(b, heads, s, ts, dv):
    spec = pl.BlockSpec((1, heads, 1, dv + VT_PAD, ts), lambda i, j: (i, 0, j, 0, 0))
    return spec, jax.ShapeDtypeStruct((b, heads, s // ts, dv + VT_PAD, ts), BF16)


def _diff_prep_body(q_ref, k_ref, v_ref, qg_ref, kg_ref, c_ref, s1_ref, s2_ref, qo_ref, ko_ref, vt_ref):
    lane = lax.broadcasted_iota(jnp.int32, (1, LANES), 1)
    low = lane < DIFF_QK_DIM
    scale = DIFF_QK_DIM ** -0.5 * LOG2E
    nh = DIFF_HEADS

    def norm_rope(x, g_ref):
        x2 = x * x
        ss_lo = jnp.sum(jnp.where(low, x2, 0.0), axis=-1, keepdims=True)
        ss_hi = jnp.sum(jnp.where(low, 0.0, x2), axis=-1, keepdims=True)
        ms = jnp.where(low, ss_lo, ss_hi) * (1.0 / DIFF_QK_DIM)
        y = x * lax.rsqrt(ms + EPS) * g_ref[...]
        return _rope(y, c_ref, s1_ref, s2_ref, DIFF_ROPE_DIM // 2)

    for h in range(nh):
        cols = slice(h * LANES, (h + 1) * LANES)
        q = norm_rope(q_ref[0, :, cols].astype(F32), qg_ref) * scale
        qo_ref[0, :, cols] = jnp.where(low, q, 0.0).astype(BF16)
        qo_ref[0, :, nh * LANES + h * LANES:nh * LANES + (h + 1) * LANES] = jnp.where(low, 0.0, q).astype(BF16)
        ko_ref[0, :, cols] = norm_rope(k_ref[0, :, cols].astype(F32), kg_ref).astype(BF16)
        _store_vt(vt_ref, h, v_ref[0, :, cols].astype(F32))


def diff_prep(u, q_gain, k_gain, tables, q_col, k_col, v_col, ts):
    b, s, _ = u.shape
    w = DIFF_HEADS * LANES
    tab = pl.BlockSpec((ts, LANES), lambda i, j: (j, 0))
    vec = pl.BlockSpec((1, LANES), lambda i, j: (0, 0))
    vt_spec, vt_shape = _vt_spec_shape(b, DIFF_HEADS, s, ts, LANES)
    return pl.pallas_call(
        _diff_prep_body,
        grid=(b, s // ts),
        in_specs=[pl.BlockSpec((1, ts, w), lambda i, j: (i, j, q_col)),
                  pl.BlockSpec((1, ts, w), lambda i, j: (i, j, k_col)),
                  pl.BlockSpec((1, ts, w), lambda i, j: (i, j, v_col)),
                  vec, vec, tab, tab, tab],
        out_specs=[pl.BlockSpec((1, ts, 2 * w), lambda i, j: (i, j, 0)),
                   pl.BlockSpec((1, ts, w), lambda i, j: (i, j, 0)), vt_spec],
        out_shape=[jax.ShapeDtypeStruct((b, s, 2 * w), BF16), jax.ShapeDtypeStruct((b, s, w), BF16), vt_shape],
        compiler_params=_params("parallel", "parallel"),
        name="diff_prep",
    )(u, u, u, q_gain, k_gain, *tables)


def _gqa_prep_body(q_ref, k_ref, v_ref, qg_ref, kg_ref, c_ref, s1_ref, s2_ref, qo_ref, ko_ref, vt_ref):
    scale = HEAD_DIM ** -0.5 * LOG2E
    for h in range(GQA_Q_HEADS):
        cols = slice(h * LANES, (h + 1) * LANES)
        y = _rms(q_ref[0, :, cols].astype(F32), qg_ref[...])
        qo_ref[0, :, cols] = (_rope(y, c_ref, s1_ref, s2_ref, HEAD_DIM // 4) * scale).astype(BF16)
    for h in range(GQA_KV_HEADS):
        cols = slice(h * LANES, (h + 1) * LANES)
        y = _rms(k_ref[0, :, cols].astype(F32), kg_ref[...])
        ko_ref[0, :, cols] = _rope(y, c_ref, s1_ref, s2_ref, HEAD_DIM // 4).astype(BF16)
        _store_vt(vt_ref, h, v_ref[0, :, cols].astype(F32))


def gqa_prep(u, q_gain, k_gain, tables, q_col, k_col, v_col, ts):
    b, s, _ = u.shape
    wq = GQA_Q_HEADS * HEAD_DIM
    wk = GQA_KV_HEADS * HEAD_DIM
    tab = pl.BlockSpec((ts, LANES), lambda i, j: (j, 0))
    vec = pl.BlockSpec((1, LANES), lambda i, j: (0, 0))
    vt_spec, vt_shape = _vt_spec_shape(b, GQA_KV_HEADS, s, ts, HEAD_DIM)
    return pl.pallas_call(
        _gqa_prep_body,
        grid=(b, s // ts),
        in_specs=[pl.BlockSpec((1, ts, wq), lambda i, j: (i, j, q_col)),
                  pl.BlockSpec((1, ts, wk), lambda i, j: (i, j, k_col)),
                  pl.BlockSpec((1, ts, wk), lambda i, j: (i, j, v_col)),
                  vec, vec, tab, tab, tab],
        out_specs=[pl.BlockSpec((1, ts, wq), lambda i, j: (i, j, 0)),
                   pl.BlockSpec((1, ts, wk), lambda i, j: (i, j, 0)), vt_spec],
        out_shape=[jax.ShapeDtypeStruct((b, s, wq), BF16), jax.ShapeDtypeStruct((b, s, wk), BF16), vt_shape],
        compiler_params=_params("parallel", "parallel"),
        name="gqa_prep",
    )(u, u, u, q_gain, k_gain, *tables)


def _mla_prep_body(cq_ref, ckv_ref, ckr_ref, cqg_ref, ckvg_ref, wq_ref, wk_ref, wr_ref, wv_ref,
                   qg_ref, kg_ref, c_ref, s1_ref, s2_ref, qo_ref, ko_ref, vt_ref):
    scale = MLA_QK_DIM ** -0.5 * LOG2E
    cq = _rms(cq_ref[0].astype(F32), cqg_ref[...]).astype(BF16)
    ckv = _rms(ckv_ref[0].astype(F32), ckvg_ref[...]).astype(BF16)
    q = _dot(cq, wq_ref[...])
    k = _dot(ckv, wk_ref[...]) + _dot(ckr_ref[0], wr_ref[...])
    v = _dot(ckv, wv_ref[...])

    def norm_rope(x, g_ref):
        ms = jnp.sum(x * x, axis=-1, keepdims=True) * (1.0 / MLA_QK_DIM)
        y = x * lax.rsqrt(ms + EPS) * g_ref[...]
        return _rope(y, c_ref, s1_ref, s2_ref, MLA_ROPE_DIM // 2)

    for h in range(MLA_HEADS):
        cols = slice(h * MLA_PAD_DIM, (h + 1) * MLA_PAD_DIM)
        qo_ref[0, :, cols] = (norm_rope(q[:, cols], qg_ref) * scale).astype(BF16)
        ko_ref[0, :, cols] = norm_rope(k[:, cols], kg_ref).astype(BF16)
        _store_vt(vt_ref, h, v[:, h * MLA_V_DIM:(h + 1) * MLA_V_DIM])


def mla_prep(u, cq_gain, ckv_gain, w_q, w_k, w_r, w_v, q_gain, k_gain, tables, cq_col, ckv_col, ckr_col, ts):
    b, s, _ = u.shape
    q_rank, kv_rank = w_q.shape[0], w_k.shape[0]
    wqk = MLA_HEADS * MLA_PAD_DIM
    tab = pl.BlockSpec((ts, MLA_PAD_DIM), lambda i, j: (j, 0))
    vt_spec, vt_shape = _vt_spec_shape(b, MLA_HEADS, s, ts, MLA_V_DIM)

    def full(a):
        return pl.BlockSpec(a.shape, lambda i, j: (0,) * a.ndim)

    return pl.pallas_call(
        _mla_prep_body,
        grid=(b, s // ts),
        in_specs=[pl.BlockSpec((1, ts, q_rank), lambda i, j: (i, j, cq_col)),
                  pl.BlockSpec((1, ts, kv_rank), lambda i, j: (i, j, ckv_col)),
                  pl.BlockSpec((1, ts, LANES), lambda i, j: (i, j, ckr_col)),
                  full(cq_gain), full(ckv_gain), full(w_q), full(w_k), full(w_r), full(w_v),
                  full(q_gain), full(k_gain), tab, tab, tab],
        out_specs=[pl.BlockSpec((1, ts, wqk), lambda i, j: (i, j, 0)),
                   pl.BlockSpec((1, ts, wqk), lambda i, j: (i, j, 0)), vt_spec],
        out_shape=[jax.ShapeDtypeStruct((b, s, wqk), BF16), jax.ShapeDtypeStruct((b, s, wqk), BF16), vt_shape],
        compiler_params=_params("parallel", "parallel"),
        name="mla_prep",
    )(u, u, u, cq_gain, ckv_gain, w_q, w_k, w_r, w_v, q_gain, k_gain, *tables)


def _flash_body(q_ref, k_ref, vt_ref, o_ref, sa_scr, sb_scr, acc_scr, *, tk, n_split):
    nk = k_ref.shape[1] // tk
    q = q_ref[0]
    tq = q.shape[0]
    dv = o_ref.shape[2]
    hw = tq // n_split

    def scores(c):
        k = k_ref[0, pl.ds(pl.multiple_of(c * tk, tk), tk), :]
        return lax.dot_general(k, q, (((1,), (1,)), ((), ())), preferred_element_type=F32)

    def softmax_pv(c, s_ref, m):
        m_out = []
        for h in range(n_split):
            cols = slice(h * hw, (h + 1) * hw)
            s = s_ref[:, cols]
            m_new = jnp.maximum(m[h], jnp.max(s, axis=0, keepdims=True))
            alpha = jnp.exp2(m[h] - m_new)
            p = jnp.exp2(s - m_new).astype(BF16)
            acc_scr[:, cols] = alpha * acc_scr[:, cols] + _dot(vt_ref[0, 0, c], p)
            m_out.append(m_new)
        return tuple(m_out)

    def pair(j, m):
        c = 2 * j
        sb_scr[...] = scores(c + 1)
        m = softmax_pv(c, sa_scr, m)
        sa_scr[...] = scores(c + 2)
        return softmax_pv(c + 1, sb_scr, m)

    acc_scr[...] = jnp.zeros(acc_scr.shape, F32)
    sa_scr[...] = scores(0)
    m = tuple(jnp.full((1, hw), -jnp.inf, F32) for _ in range(n_split))
    m = lax.fori_loop(0, nk // 2 - 1, pair, m, unroll=3)
    sb_scr[...] = scores(nk - 1)
    m = softmax_pv(nk - 2, sa_scr, m)
    softmax_pv(nk - 1, sb_scr, m)
    acc = acc_scr[...]
    o_ref[0] = (acc[0:dv, :] / acc[dv:dv + 1, :]).T.astype(o_ref.dtype)


def flash_attention(q, k, vt, n_maps, dk, q_col, k_col, v_head, out_dtype, tq):
    b, s, _ = q.shape
    _, _, nk, dva, tk = vt.shape
    dv = dva - VT_PAD
    return pl.pallas_call(
        functools.partial(_flash_body, tk=tk, n_split=max(1, tq // MXU_WIDTH)),
        grid=(b, n_maps, s // tq),
        in_specs=[pl.BlockSpec((1, tq, dk), lambda i, h, a: (i, a, q_col(h))),
                  pl.BlockSpec((1, s, dk), lambda i, h, a: (i, 0, k_col(h))),
                  pl.BlockSpec((1, 1, nk, dva, tk), lambda i, h, a: (i, v_head(h), 0, 0, 0))],
        out_specs=pl.BlockSpec((1, tq, dv), lambda i, h, a: (i, a, h)),
        out_shape=jax.ShapeDtypeStruct((b, s, n_maps * dv), out_dtype),
        scratch_shapes=[pltpu.VMEM((tk, tq), F32), pltpu.VMEM((tk, tq), F32), pltpu.VMEM((dva, tq), F32)],
        compiler_params=_params("parallel", "parallel", "arbitrary"),
        name="flash_attention",
    )(q, k, vt)


def _diff_combine_body(o_ref, lam_ref, g_ref, y_ref, *, lambda_init):
    lam = lam_ref[...]
    lam_full = (jnp.exp(jnp.sum(lam[0:1] * lam[1:2], keepdims=True))
                - jnp.exp(jnp.sum(lam[2:3] * lam[3:4], keepdims=True)) + lambda_init)
    w = DIFF_HEADS * LANES
    for h in range(DIFF_HEADS):
        cols = slice(h * LANES, (h + 1) * LANES)
        d = o_ref[0, :, cols] - lam_full * o_ref[0, :, w + h * LANES:w + (h + 1) * LANES]
        y_ref[0, :, cols] = (_rms(d, g_ref[...]) * (1.0 - lambda_init)).astype(y_ref.dtype)


def diff_combine(o, lam, out_gain, lambda_init, ts):
    b, s, w2 = o.shape
    w = w2 // 2
    return pl.pallas_call(
        functools.partial(_diff_combine_body, lambda_init=lambda_init),
        grid=(b, s // ts),
        in_specs=[pl.BlockSpec((1, ts, w2), lambda i, j: (i, j, 0)),
                  pl.BlockSpec(lam.shape, lambda i, j: (0, 0)),
                  pl.BlockSpec((1, LANES), lambda i, j: (0, 0))],
        out_specs=pl.BlockSpec((1, ts, w), lambda i, j: (i, j, 0)),
        out_shape=jax.ShapeDtypeStruct((b, s, w), BF16),
        compiler_params=_params("parallel", "parallel"),
        name="diff_combine",
    )(o, lam, out_gain)


def _merge_body(xn_ref, ya_ref, yb_ref, yc_ref, yd_ref, wg0, wg1, wg2, wg3, p0, p1, p2, p3, o_ref):
    xn = xn_ref[...]
    acc = None
    for y_ref, wg_ref, p_ref in ((ya_ref, wg0, p0), (yb_ref, wg1, p1), (yc_ref, wg2, p2), (yd_ref, wg3, p3)):
        term = _sigmoid(_dot(xn, wg_ref[...])) * _dot(y_ref[...], p_ref[0])
        acc = term if acc is None else acc + term
    o_ref[...] = acc.astype(o_ref.dtype)


def gated_merge(xn, ys, w_gate, w_branch, tm, tn):
    m, d = xn.shape
    width = w_branch.shape[1]
    nj = d // tn
    y_spec = pl.BlockSpec((tm, width), lambda i, j: (i, 0))
    wg_specs = [pl.BlockSpec((d, tn), functools.partial(lambda i, j, br: (0, br * nj + j), br=br)) for br in range(4)]
    p_specs = [pl.BlockSpec((1, width, tn), functools.partial(lambda i, j, br: (br, 0, j), br=br)) for br in range(4)]
    return pl.pallas_call(
        _merge_body,
        grid=(m // tm, nj),
        in_specs=[pl.BlockSpec((tm, d), lambda i, j: (i, 0)), y_spec, y_spec, y_spec, y_spec] + wg_specs + p_specs,
        out_specs=pl.BlockSpec((tm, tn), lambda i, j: (i, j)),
        out_shape=jax.ShapeDtypeStruct((m, d), BF16),
        compiler_params=_params("parallel", "arbitrary"),
        name="gated_merge",
    )(xn, *ys, w_gate, w_gate, w_gate, w_gate, w_branch, w_branch, w_branch, w_branch)


def _router_body(x_ref, g_ref, r_ref, idx_ref, w_ref, *, n_experts):
    xn = _rms(x_ref[...], g_ref[...])
    logits = jnp.dot(xn, r_ref[...], precision=lax.Precision.HIGHEST, preferred_element_type=F32)
    lane = lax.broadcasted_iota(jnp.int32, logits.shape, 1).astype(F32)
    neg = -jnp.inf
    lg = jnp.where(lane < n_experts, logits, neg)
    v1 = jnp.max(lg, axis=-1, keepdims=True)
    i1 = jnp.min(jnp.where(lg == v1, lane, float(LANES)), axis=-1, keepdims=True)
    lg2 = jnp.where(lane == i1, neg, lg)
    v2 = jnp.max(lg2, axis=-1, keepdims=True)
    i2 = jnp.min(jnp.where(lg2 == v2, lane, float(LANES)), axis=-1, keepdims=True)
    e21 = jnp.exp(v2 - v1)
    w1 = 1.0 / (1.0 + e21)
    w2 = e21 * w1
    idx_ref[...] = jnp.where(lane == 0.0, i1, jnp.where(lane == 1.0, i2, 0.0)).astype(jnp.int32)
    w_ref[...] = jnp.where(lane == 0.0, w1, jnp.where(lane == 1.0, w2, 0.0))


def moe_router(x, g, router_padded, n_experts, tm):
    m, d = x.shape
    return pl.pallas_call(
        functools.partial(_router_body, n_experts=n_experts),
        grid=(m // tm,),
        in_specs=[pl.BlockSpec((tm, d), lambda i: (i, 0)),
                  pl.BlockSpec((1, d), lambda i: (0, 0)),
                  pl.BlockSpec((d, LANES), lambda i: (0, 0))],
        out_specs=[pl.BlockSpec((tm, LANES), lambda i: (i, 0)), pl.BlockSpec((tm, LANES), lambda i: (i, 0))],
        out_shape=[jax.ShapeDtypeStruct((m, LANES), jnp.int32), jax.ShapeDtypeStruct((m, LANES), F32)],
        compiler_params=_params("parallel"),
        name="moe_router",
    )(x, g, router_padded)


def _moe_up_body(te_ref, nv_ref, tok_ref, x_hbm, g_ref, wg_ref, wu_ref, o_ref, xbuf, xn_scr, sem, *, rows_per_step):
    t = pl.program_id(0)
    f = pl.program_id(1)
    tm = xn_scr.shape[0]
    nt = pl.num_programs(0)
    nf = pl.num_programs(1)
    n_rows = rows_per_step * nf
    slot = t % 2

    def row_copy(dst_slot, r, tok):
        return pltpu.make_async_copy(x_hbm.at[pl.ds(tok, 1)], xbuf.at[dst_slot, pl.ds(r, 1)], sem.at[dst_slot])

    def prefetch_next_tile():
        for i in range(rows_per_step):
            r = f * rows_per_step + i
            row_copy(1 - slot, r, tok_ref[(t + 1) * tm + r]).start()

    def wait_rows(dst_slot):
        def wait(r, c):
            row_copy(dst_slot, r, 0).wait()
            return c
        lax.fori_loop(0, n_rows, wait, 0)

    @pl.when(jnp.logical_and(t == 0, f == 0))
    def _():
        def start(r, c):
            row_copy(0, r, tok_ref[r]).start()
            return c
        lax.fori_loop(0, n_rows, start, 0)

    @pl.when(f == 0)
    def _():
        wait_rows(slot)

    @pl.when(t < nv_ref[0])
    def _():
        @pl.when(f == 0)
        def _():
            xn_scr[...] = _rms(xbuf[slot, 0:tm, :], g_ref[...]).astype(BF16)

        prefetch_next_tile()
        _swiglu_columns(xn_scr[...], wg_ref.at[0], wu_ref.at[0], o_ref)

    @pl.when(t >= nv_ref[0])
    def _():
        prefetch_next_tile()
        o_ref[...] = jnp.zeros(o_ref.shape, o_ref.dtype)

    @pl.when(jnp.logical_and(t == nt - 1, f == nf - 1))
    def _():
        wait_rows(1 - slot)


def moe_up(tile_expert, n_valid, row_token, x, g, w_gate_up, tm, tf):
    d = x.shape[1]
    f = w_gate_up.shape[2] // 2
    nf = f // tf
    p = row_token.shape[0] - 2 * tm
    rows_per_step = -(-tm // nf)
    buf_rows = -(-rows_per_step * nf // 8) * 8

    def wmap(off, t, j, te, nv, tok):
        return (te[t], 0, off + jnp.where(t < nv[0], j, nf - 1))

    grid_spec = pltpu.PrefetchScalarGridSpec(
        num_scalar_prefetch=3,
        grid=(p // tm, nf),
        in_specs=[pl.BlockSpec(memory_space=pl.ANY),
                  pl.BlockSpec((1, d), lambda t, j, te, nv, tok: (0, 0)),
                  pl.BlockSpec((1, d, tf), functools.partial(wmap, 0)),
                  pl.BlockSpec((1, d, tf), functools.partial(wmap, nf))],
        out_specs=pl.BlockSpec((tm, tf), lambda t, j, te, nv, tok: (t, j)),
        scratch_shapes=[pltpu.VMEM((2, buf_rows, d), F32), pltpu.VMEM((tm, d), BF16), pltpu.SemaphoreType.DMA((2,))],
    )
    return pl.pallas_call(
        functools.partial(_moe_up_body, rows_per_step=rows_per_step),
        grid_spec=grid_spec,
        out_shape=jax.ShapeDtypeStruct((p, f), BF16),
        compiler_params=_params("arbitrary", "arbitrary"),
        name="moe_up",
    )(tile_expert, n_valid, row_token, x, g, w_gate_up, w_gate_up)


def _moe_down_body(te_ref, nv_ref, h_ref, w_ref, o_ref):
    t = pl.program_id(0)

    @pl.when(t < nv_ref[0])
    def _():
        o_ref[...] = _dot(h_ref[...], w_ref[0])

    @pl.when(t >= nv_ref[0])
    def _():
        o_ref[...] = jnp.zeros(o_ref.shape, o_ref.dtype)


def moe_down(tile_expert, n_valid, h, w_down, tm, tn):
    p, f = h.shape
    d = w_down.shape[2]
    nj = d // tn
    grid_spec = pltpu.PrefetchScalarGridSpec(
        num_scalar_prefetch=2,
        grid=(p // tm, nj),
        in_specs=[pl.BlockSpec((tm, f), lambda t, j, te, nv: (t, 0)),
                  pl.BlockSpec((1, f, tn), lambda t, j, te, nv: (te[t], 0, jnp.where(t < nv[0], j, nj - 1)))],
        out_specs=pl.BlockSpec((tm, tn), lambda t, j, te, nv: (t, j)),
    )
    return pl.pallas_call(
        _moe_down_body,
        grid_spec=grid_spec,
        out_shape=jax.ShapeDtypeStruct((p, d), F32),
        compiler_params=_params("arbitrary", "arbitrary"),
        name="moe_down",
    )(tile_expert, n_valid, h, w_down)


def _moe_combine_body(pos_ref, x_ref, w_ref, y_hbm, o_ref, buf0, buf1, sem):
    i = pl.program_id(0)
    tc = buf0.shape[0]

    def row_copy(r, k, src):
        dst = buf0 if k == 0 else buf1
        return pltpu.make_async_copy(y_hbm.at[pl.ds(src, 1)], dst.at[pl.ds(r, 1)], sem)

    def start(r, c):
        for k in range(TOP_K):
            row_copy(r, k, pos_ref[(i * tc + r) * TOP_K + k]).start()
        return c

    def wait(r, c):
        for k in range(TOP_K):
            row_copy(r, k, 0).wait()
        return c

    lax.fori_loop(0, tc, start, 0, unroll=8)
    lax.fori_loop(0, tc, wait, 0, unroll=8)
    o_ref[...] = x_ref[...] + w_ref[:, 0:1] * buf0[...] + w_ref[:, 1:2] * buf1[...]


def moe_combine(pos, x, wts, y, tc):
    m, d = x.shape
    grid_spec = pltpu.PrefetchScalarGridSpec(
        num_scalar_prefetch=1,
        grid=(m // tc,),
        in_specs=[pl.BlockSpec((tc, d), lambda i, pos: (i, 0)),
                  pl.BlockSpec((tc, LANES), lambda i, pos: (i, 0)),
                  pl.BlockSpec(memory_space=pl.ANY)],
        out_specs=pl.BlockSpec((tc, d), lambda i, pos: (i, 0)),
        scratch_shapes=[pltpu.VMEM((tc, d), F32), pltpu.VMEM((tc, d), F32), pltpu.SemaphoreType.DMA(())],
    )
    return pl.pallas_call(
        _moe_combine_body,
        grid_spec=grid_spec,
        out_shape=jax.ShapeDtypeStruct((m, d), F32),
        compiler_params=_params("arbitrary"),
        name="moe_combine",
    )(pos, x, wts, y)


def _moe_schedule(idx, n_experts, tm):
    m = idx.shape[0]
    n_assign = m * TOP_K
    e_flat = idx.reshape(n_assign)
    onehot = (e_flat[:, None] == jnp.arange(n_experts, dtype=jnp.int32)[None, :]).astype(jnp.int32)
    csum = jnp.cumsum(onehot, axis=0)
    rank = jnp.take_along_axis(csum, e_flat[:, None], axis=1)[:, 0] - 1
    counts = csum[-1]
    padded = ((counts + tm - 1) // tm) * tm
    ends = jnp.cumsum(padded)
    pos = (ends - padded)[e_flat] + rank
    n_rows = n_assign + n_experts * tm
    row_token = jnp.zeros((n_rows + 2 * tm,), jnp.int32).at[pos].set(jnp.arange(n_assign, dtype=jnp.int32) // TOP_K)
    n_tiles = n_rows // tm
    tiles = jnp.arange(n_tiles, dtype=jnp.int32)
    n_valid = (ends[-1] // tm).astype(jnp.int32)
    tile_expert = jnp.sum((tiles[:, None] * tm >= ends[None, :]).astype(jnp.int32), axis=1)
    tile_expert = jnp.minimum(tile_expert, n_experts - 1)
    tile_expert = jnp.where(tiles < n_valid, tile_expert, tile_expert[n_valid - 1])
    return pos.astype(jnp.int32), row_token, tile_expert, n_valid.reshape(1)


def moe_ffn(x, g, router, w_gate_up, w_down):
    m, d = x.shape
    n_experts = router.shape[1]
    tm = _tile(m, 512)
    router_padded = jnp.pad(router, ((0, 0), (0, LANES - n_experts)))
    idx, wts = moe_router(x, g, router_padded, n_experts, _tile(m, 512))
    pos, row_token, tile_expert, n_valid = _moe_schedule(idx[:, :TOP_K], n_experts, tm)
    f = w_down.shape[1]
    tf = next((c for c in (7 * MXU_WIDTH, 4 * MXU_WIDTH, 2 * MXU_WIDTH) if f % c == 0), f)
    h = moe_up(tile_expert, n_valid, row_token, x, g, w_gate_up, tm, tf)
    y = moe_down(tile_expert, n_valid, h, w_down, tm, _tile(d, 512))
    return moe_combine(pos, x, wts, y, _tile(m, 256))


def _rope_cos_sin(pos, dim, theta):
    inv = jnp.power(jnp.float32(theta), -jnp.arange(0, dim, 2, dtype=F32) / dim)
    ang = pos.astype(F32)[:, None] * inv[None, :]
    return jnp.cos(ang), jnp.sin(ang)


def _rope_tables(s):
    pos = jnp.arange(s, dtype=jnp.int32)
    rows = s // GRID_W
    row_pos = jnp.broadcast_to(jnp.arange(rows, dtype=jnp.int32)[:, None], (rows, GRID_W)).reshape(-1)
    col_pos = jnp.broadcast_to(jnp.arange(GRID_W, dtype=jnp.int32)[None, :], (rows, GRID_W)).reshape(-1)

    def ones(n):
        return jnp.ones((s, n), F32)

    def zeros(n):
        return jnp.zeros((s, n), F32)

    cat = functools.partial(jnp.concatenate, axis=1)
    cos, sin = _rope_cos_sin(pos, DIFF_ROPE_DIM, ROPE_THETA)
    rest = DIFF_QK_DIM - DIFF_ROPE_DIM
    half = DIFF_ROPE_DIM // 2
    diff = (cat([cos, cos, ones(rest)] * 2), cat([-sin, zeros(half), zeros(rest)] * 2),
            cat([zeros(half), sin, zeros(rest)] * 2))
    cos, sin = _rope_cos_sin(pos, MLA_ROPE_DIM, ROPE_THETA)
    half = MLA_ROPE_DIM // 2
    tail = MLA_PAD_DIM - MLA_QK_DIM
    mla = (cat([ones(MLA_NOPE_DIM), cos, cos, ones(tail)]),
           cat([zeros(MLA_NOPE_DIM), -sin, zeros(half), zeros(tail)]),
           cat([zeros(MLA_NOPE_DIM), zeros(half), sin, zeros(tail)]))
    cos_r, sin_r = _rope_cos_sin(row_pos, HEAD_DIM // 2, AXIAL_THETA)
    cos_c, sin_c = _rope_cos_sin(col_pos, HEAD_DIM // 2, AXIAL_THETA)
    half = HEAD_DIM // 4
    gqa = (cat([cos_r, cos_r, cos_c, cos_c]), cat([-sin_r, zeros(half), -sin_c, zeros(half)]),
           cat([zeros(half), sin_r, zeros(half), sin_c]))
    return diff, mla, gqa


COL_A_GATE, COL_A_X, COL_B_Q, COL_B_K, COL_B_V, COL_C_Q, COL_D_Q = (i * 512 for i in range(7))
COL_C_KV, COL_D_K, COL_D_V, COL_C_KR = 3584, 3840, 4096, 4352
MIX_COLS = 4608
MIX_TN = 3 * MXU_WIDTH


def _mixer_in_weights(w_in_l):
    widths = (512, 512, 512, 512, 512, 512, 256, 64, 512, 256, 256)
    offs = [0]
    for wd in widths:
        offs.append(offs[-1] + wd)
    (a_gate, a_x, b_q, b_k, b_v, c_q, c_kv, c_kr, d_q, d_k, d_v) = (
        w_in_l[:, offs[i]:offs[i + 1]] for i in range(len(widths)))
    pad = jnp.zeros((w_in_l.shape[0], MIX_COLS - COL_C_KR - MLA_ROPE_DIM), w_in_l.dtype)
    w_mix = jnp.concatenate([a_gate, a_x, b_q, b_k, b_v, c_q, d_q, c_kv, d_k, d_v, c_kr, pad], axis=1)
    return w_mix.astype(BF16), w_in_l[:, offs[-1]:].astype(BF16)


def _mla_weights(w_uq, w_ukv):
    q_rank, kv_rank = w_uq.shape[0], w_ukv.shape[0]
    wq = w_uq.reshape(q_rank, MLA_HEADS, MLA_QK_DIM)
    wq = jnp.pad(wq, ((0, 0), (0, 0), (0, MLA_PAD_DIM - MLA_QK_DIM))).reshape(q_rank, MLA_HEADS * MLA_PAD_DIM)
    wkv = w_ukv.reshape(kv_rank, MLA_HEADS, MLA_NOPE_DIM + MLA_V_DIM)
    wk = jnp.pad(wkv[:, :, :MLA_NOPE_DIM], ((0, 0), (0, 0), (0, MLA_PAD_DIM - MLA_NOPE_DIM)))
    wk = wk.reshape(kv_rank, MLA_HEADS * MLA_PAD_DIM)
    wv = wkv[:, :, MLA_NOPE_DIM:].reshape(kv_rank, MLA_HEADS * MLA_V_DIM)
    place = jnp.zeros((LANES, MLA_HEADS, MLA_PAD_DIM), F32)
    r = jnp.arange(MLA_ROPE_DIM)
    place = place.at[r, :, MLA_NOPE_DIM + r].set(1.0).reshape(LANES, MLA_HEADS * MLA_PAD_DIM)
    return wq.astype(BF16), wk.astype(BF16), place.astype(BF16), wv.astype(BF16)


def _lru_gate_weights(gate_w):
    nb = gate_w.shape[2]
    per = LANES // LRU_BLOCK_DIM
    w = gate_w.reshape(4, nb // per, per, LRU_BLOCK_DIM, LRU_BLOCK_DIM)
    bd = jnp.einsum("dcpkj,pq->cdpkqj", w, jnp.eye(per, dtype=w.dtype))
    return bd.reshape(nb // per, 4, LANES, LANES).astype(BF16)


def _pad_gain(g, width):
    return jnp.pad(g, (0, width - g.shape[0])).reshape(1, width)


def kernel(x, norm_mix, w_in, lru_conv_w, lru_conv_b, lru_gate_w, lru_gate_b, lru_lambda, diff_q_norm, diff_k_norm, diff_lambda, diff_out_norm, mla_cq_norm, mla_ckv_norm, mla_w_uq, mla_w_ukv, mla_q_norm, mla_k_norm, gqa_q_norm, gqa_k_norm, w_branch, w_out, norm_ffn, ffn_w_gate_up, ffn_w_down, moe_router, moe_w_gate_up, moe_w_down):
    b, s, d = x.shape
    m = b * s
    depth = w_in.shape[0]
    lru_width = lru_conv_w.shape[2]
    tables_diff, tables_mla, tables_gqa = _rope_tables(s)
    ts = 512 if s % 1024 == 0 else s // 2
    tq = _tile(s, 1024)
    tm = _tile(m, 512)
    tm_big = _tile(m, 1024)
    xf = x.reshape(m, d)

    for l in range(depth):
        w_mix, w_gate = _mixer_in_weights(w_in[l])
        u, xn = norm_proj(xf, norm_mix[l].reshape(1, d), w_mix, tm_big, MIX_TN)
        u = u.reshape(b, s, MIX_COLS)

        y_a = lru_mixer(u, lru_conv_w[l], lru_conv_b[l].reshape(1, lru_width), _lru_gate_weights(lru_gate_w[l]),
                        lru_gate_b[l].reshape(4, lru_width), lru_lambda[l], COL_A_GATE // LANES, COL_A_X // LANES)

        lambda_init = 0.8 - 0.6 * math.exp(-0.3 * l)
        q_b, k_b, vt_b = diff_prep(u, jnp.tile(diff_q_norm[l], 2).reshape(1, LANES),
                                   jnp.tile(diff_k_norm[l], 2).reshape(1, LANES),
                                   tables_diff, COL_B_Q // 512, COL_B_K // 512, COL_B_V // 512, ts)
        o_b = flash_attention(q_b, k_b, vt_b, 2 * DIFF_HEADS, LANES,
                              lambda h: h, lambda h: h % DIFF_HEADS, lambda h: h % DIFF_HEADS, F32, tq)
        y_b = diff_combine(o_b, diff_lambda[l], diff_out_norm[l].reshape(1, LANES), lambda_init, ts)

        wq, wk, wr, wv = _mla_weights(mla_w_uq[l], mla_w_ukv[l])
        q_c, k_c, vt_c = mla_prep(u, mla_cq_norm[l].reshape(1, -1), mla_ckv_norm[l].reshape(1, -1), wq, wk, wr, wv,
                                 _pad_gain(mla_q_norm[l], MLA_PAD_DIM), _pad_gain(mla_k_norm[l], MLA_PAD_DIM),
                                 tables_mla, COL_C_Q // 512, COL_C_KV // 256, COL_C_KR // LANES, ts)
        y_c = flash_attention(q_c, k_c, vt_c, MLA_HEADS, MLA_PAD_DIM,
                              lambda h: h, lambda h: h, lambda h: h, BF16, tq)

        group = GQA_Q_HEADS // GQA_KV_HEADS
        q_d, k_d, vt_d = gqa_prep(u, gqa_q_norm[l].reshape(1, LANES), gqa_k_norm[l].reshape(1, LANES), tables_gqa,
                                  COL_D_Q // 512, COL_D_K // 256, COL_D_V // 256, ts)
        y_d = flash_attention(q_d, k_d, vt_d, GQA_Q_HEADS, HEAD_DIM,
                              lambda h: h, lambda h: h // group, lambda h: h // group, BF16, tq)

        ys = [y.reshape(m, -1) for y in (y_a, y_b, y_c, y_d)]
        merged = gated_merge(xn, ys, w_gate, w_branch[l].astype(BF16), tm, _tile(d, 512))
        xf = mm_residual(merged, w_out[l].astype(BF16), xf, tm_big, _tile(d, 512))

        g_ffn = norm_ffn[l].reshape(1, d)
        if l % 2 == 0:
            f = ffn_w_down.shape[1]
            act = norm_swiglu(xf, g_ffn, ffn_w_gate_up[l // 2].astype(BF16), tm_big, _tile(f, 512))
            xf = mm_residual(act, ffn_w_down[l // 2].astype(BF16), xf, tm_big, _tile(d, 512))
        else:
            xf = moe_ffn(xf, g_ffn, moe_router[l // 2], moe_w_gate_up[l // 2].astype(BF16),
                         moe_w_down[l // 2].astype(BF16))
    return xf.reshape(b, s, d)
```

```python
import functools
import math

import jax
import jax.numpy as jnp
from jax import lax
from jax.experimental import pallas as pl
from jax.experimental.pallas import tpu as pltpu

F32 = jnp.float32
BF16 = jnp.bfloat16
EPS = 1e-6
LANES = 128
MXU_WIDTH = 256

LRU_BLOCK_DIM = 64
LRU_C = 8.0
DIFF_HEADS = 4
DIFF_QK_DIM = 64
DIFF_ROPE_DIM = 16
MLA_HEADS = 4
MLA_NOPE_DIM = 128
MLA_ROPE_DIM = 64
MLA_QK_DIM = MLA_NOPE_DIM + MLA_ROPE_DIM
MLA_PAD_DIM = 256
MLA_V_DIM = 128
GQA_Q_HEADS = 4
GQA_KV_HEADS = 2
HEAD_DIM = 128
GRID_W = 64
ROPE_THETA = 500000.0
AXIAL_THETA = 10000.0
TOP_K = 2
LOG2E = 1.4426950408889634
VT_PAD = 16

VMEM_LIMIT = 56 * 1024 * 1024


def _params(*sem):
    return pltpu.CompilerParams(dimension_semantics=sem, vmem_limit_bytes=VMEM_LIMIT)


def _tile(dim, pref):
    return pref if dim % pref == 0 else dim


def _sigmoid(x):
    return 1.0 / (1.0 + jnp.exp(-x))


def _dot(a, b):
    return jnp.dot(a, b, preferred_element_type=F32)


def _rms(x, g):
    ms = jnp.mean(x * x, axis=-1, keepdims=True)
    return x * lax.rsqrt(ms + EPS) * g


def _norm_proj_body(x_ref, g_ref, w_ref, o_ref, xn_ref, xn_scr):
    @pl.when(pl.program_id(1) == 0)
    def _():
        xn = _rms(x_ref[...], g_ref[...]).astype(BF16)
        xn_scr[...] = xn
        xn_ref[...] = xn

    o_ref[...] = _dot(xn_scr[...], w_ref[...]).astype(o_ref.dtype)


def norm_proj(x, g, w, tm, tn):
    m, d = x.shape
    n = w.shape[1]
    return pl.pallas_call(
        _norm_proj_body,
        grid=(m // tm, n // tn),
        in_specs=[pl.BlockSpec((tm, d), lambda i, j: (i, 0)),
                  pl.BlockSpec((1, d), lambda i, j: (0, 0)),
                  pl.BlockSpec((d, tn), lambda i, j: (0, j))],
        out_specs=[pl.BlockSpec((tm, tn), lambda i, j: (i, j)),
                   pl.BlockSpec((tm, d), lambda i, j: (i, 0))],
        out_shape=[jax.ShapeDtypeStruct((m, n), BF16), jax.ShapeDtypeStruct((m, d), BF16)],
        scratch_shapes=[pltpu.VMEM((tm, d), BF16)],
        compiler_params=_params("parallel", "arbitrary"),
        name="norm_proj",
    )(x, g, w)


def _swiglu_columns(xn, wg_ref, wu_ref, o_ref):
    n = o_ref.shape[1]
    width = MXU_WIDTH if n % MXU_WIDTH == 0 else n
    for c in range(n // width):
        cols = slice(c * width, (c + 1) * width)
        gate = _dot(xn, wg_ref[:, cols])
        up = _dot(xn, wu_ref[:, cols])
        o_ref[:, cols] = (gate * _sigmoid(gate) * up).astype(o_ref.dtype)


def _norm_swiglu_body(x_ref, g_ref, wg_ref, wu_ref, o_ref, xn_scr):
    @pl.when(pl.program_id(1) == 0)
    def _():
        xn_scr[...] = _rms(x_ref[...], g_ref[...]).astype(BF16)

    _swiglu_columns(xn_scr[...], wg_ref, wu_ref, o_ref)


def norm_swiglu(x, g, w_gate_up, tm, tf):
    m, d = x.shape
    f = w_gate_up.shape[1] // 2
    nf = f // tf
    return pl.pallas_call(
        _norm_swiglu_body,
        grid=(m // tm, nf),
        in_specs=[pl.BlockSpec((tm, d), lambda i, j: (i, 0)),
                  pl.BlockSpec((1, d), lambda i, j: (0, 0)),
                  pl.BlockSpec((d, tf), lambda i, j: (0, j)),
                  pl.BlockSpec((d, tf), lambda i, j: (0, nf + j))],
        out_specs=pl.BlockSpec((tm, tf), lambda i, j: (i, j)),
        out_shape=jax.ShapeDtypeStruct((m, f), BF16),
        scratch_shapes=[pltpu.VMEM((tm, d), BF16)],
        compiler_params=_params("parallel", "arbitrary"),
        name="norm_swiglu",
    )(x, g, w_gate_up, w_gate_up)


def _mm_residual_body(a_ref, b_ref, r_ref, o_ref):
    o_ref[...] = r_ref[...] + _dot(a_ref[...], b_ref[...])


def mm_residual(a, b, res, tm, tn):
    m, k = a.shape
    n = b.shape[1]
    return pl.pallas_call(
        _mm_residual_body,
        grid=(m // tm, n // tn),
        in_specs=[pl.BlockSpec((tm, k), lambda i, j: (i, 0)),
                  pl.BlockSpec((k, tn), lambda i, j: (0, j)),
                  pl.BlockSpec((tm, tn), lambda i, j: (i, j))],
        out_specs=pl.BlockSpec((tm, tn), lambda i, j: (i, j)),
        out_shape=jax.ShapeDtypeStruct((m, n), F32),
        compiler_params=_params("parallel", "arbitrary"),
        name="mm_residual",
    )(a, b, res)


def _lru_body(ug_ref, ux_ref, cw_ref, cb_ref, gw_ref, gb_ref, lam_ref, o_ref, xpad, hf, *, chunk):
    s = ux_ref.shape[1]
    t = chunk
    nc = s // t
    halo = 8
    xpad[0:halo, :] = jnp.zeros((halo, LANES), F32)
    xpad[s + halo:s + 2 * halo, :] = jnp.zeros((halo, LANES), F32)
    xpad[halo:s + halo, :] = ux_ref[0].astype(F32)

    z = -lam_ref[...]
    softplus = jnp.maximum(z, 0.0) + jnp.log(1.0 + jnp.exp(-jnp.abs(z)))
    decay = -LRU_C * softplus
    row = lax.broadcasted_iota(jnp.int32, (t, LANES), 0)

    def coeffs(c, d):
        ext = xpad[pl.ds(pl.multiple_of(c * t, 8), t + 2 * halo), :]
        xc = cb_ref[...] + sum(cw_ref[j:j + 1, :] * ext[halo - 2 + j:halo - 2 + j + t] for j in range(4))
        xcb = xc.astype(BF16)
        r = _sigmoid(_dot(xcb, gw_ref[0, 2 * d]) + gb_ref[2 * d:2 * d + 1, :])
        i = _sigmoid(_dot(xcb, gw_ref[0, 2 * d + 1]) + gb_ref[2 * d + 1:2 * d + 2, :])
        log_a = decay[d:d + 1, :] * r
        a = jnp.exp(log_a)
        b = jnp.sqrt(1.0 - jnp.exp(2.0 * log_a)) * (i * xc)
        return a, b

    def scan(a, b, reverse):
        k = 1
        while k < t:
            if reverse:
                keep, shift = row < t - k, t - k
            else:
                keep, shift = row >= k, k
            a_s = jnp.where(keep, pltpu.roll(a, shift, 0), 1.0)
            b_s = jnp.where(keep, pltpu.roll(b, shift, 0), 0.0)
            b = a * b_s + b
            a = a * a_s
            k *= 2
        return a, b

    def fwd(c, carry):
        a, b = scan(*coeffs(c, 0), reverse=False)
        h = b + a * carry
        hf[pl.ds(pl.multiple_of(c * t, 8), t), :] = h
        return h[t - 1:t, :]

    lax.fori_loop(0, nc, fwd, jnp.zeros((1, LANES), F32))

    def bwd(i, carry):
        c = nc - 1 - i
        a, b = scan(*coeffs(c, 1), reverse=True)
        h = b + a * carry
        rows = pl.ds(pl.multiple_of(c * t, 16), t)
        g = ug_ref[0, rows, :].astype(F32)
        gelu = 0.5 * g * (1.0 + jnp.tanh(0.7978845608028654 * (g + 0.044715 * (g * g * g))))
        o_ref[0, rows, :] = ((hf[rows, :] + h) * gelu).astype(o_ref.dtype)
        return h[0:1, :]

    lax.fori_loop(0, nc, bwd, jnp.zeros((1, LANES), F32))


def lru_mixer(u, conv_w, conv_b, gate_w_bd, gate_b, lam, gate_col, x_col):
    b, s, _ = u.shape
    w = conv_w.shape[1]
    chunk = _tile(s, 256)
    return pl.pallas_call(
        functools.partial(_lru_body, chunk=chunk),
        grid=(b, w // LANES),
        in_specs=[pl.BlockSpec((1, s, LANES), lambda i, c: (i, 0, gate_col + c)),
                  pl.BlockSpec((1, s, LANES), lambda i, c: (i, 0, x_col + c)),
                  pl.BlockSpec((4, LANES), lambda i, c: (0, c)),
                  pl.BlockSpec((1, LANES), lambda i, c: (0, c)),
                  pl.BlockSpec((1, 4, LANES, LANES), lambda i, c: (c, 0, 0, 0)),
                  pl.BlockSpec((4, LANES), lambda i, c: (0, c)),
                  pl.BlockSpec((2, LANES), lambda i, c: (0, c))],
        out_specs=pl.BlockSpec((1, s, LANES), lambda i, c: (i, 0, c)),
        out_shape=jax.ShapeDtypeStruct((b, s, w), BF16),
        scratch_shapes=[pltpu.VMEM((s + 16, LANES), F32), pltpu.VMEM((s, LANES), F32)],
        compiler_params=_params("parallel", "parallel"),
        name="lru_mixer",
    )(u, u, conv_w, conv_b, gate_w_bd, gate_b, lam)


def _rope(y, c_ref, s1_ref, s2_ref, shift):
    w = y.shape[-1]
    return y * c_ref[...] + pltpu.roll(y, w - shift, 1) * s1_ref[...] + pltpu.roll(y, shift, 1) * s2_ref[...]


def _store_vt(vt_ref, h, v):
    ts, dv = v.shape
    vt_ref[0, h, 0, 0:dv, :] = v.T.astype(BF16)
    vt_ref[0, h, 0, dv:dv + VT_PAD, :] = jnp.ones((VT_PAD, ts), BF16)


def _vt_spec_shape(b, heads, s, ts, dv):
    spec = pl.BlockSpec((1, heads, 1, dv + VT_PAD, ts), lambda i, j: (i, 0, j, 0, 0))
    return spec, jax.ShapeDtypeStruct((b, heads, s // ts, dv + VT_PAD, ts), BF16)


def _diff_prep_body(q_ref, k_ref, v_ref, qg_ref, kg_ref, c_ref, s1_ref, s2_ref, qo_ref, ko_ref, vt_ref):
    lane = lax.broadcasted_iota(jnp.int32, (1, LANES), 1)
    low = lane < DIFF_QK_DIM
    scale = DIFF_QK_DIM ** -0.5 * LOG2E
    nh = DIFF_HEADS

    def norm_rope(x, g_ref):
        x2 = x * x
        ss_lo = jnp.sum(jnp.where(low, x2, 0.0), axis=-1, keepdims=True)
        ss_hi = jnp.sum(jnp.where(low, 0.0, x2), axis=-1, keepdims=True)
        ms = jnp.where(low, ss_lo, ss_hi) * (1.0 / DIFF_QK_DIM)
        y = x * lax.rsqrt(ms + EPS) * g_ref[...]
        return _rope(y, c_ref, s1_ref, s2_ref, DIFF_ROPE_DIM // 2)

    for h in range(nh):
        cols = slice(h * LANES, (h + 1) * LANES)
        q = norm_rope(q_ref[0, :, cols].astype(F32), qg_ref) * scale
        qo_ref[0, :, cols] = jnp.where(low, q, 0.0).astype(BF16)
        qo_ref[0, :, nh * LANES + h * LANES:nh * LANES + (h + 1) * LANES] = jnp.where(low, 0.0, q).astype(BF16)
        ko_ref[0, :, cols] = norm_rope(k_ref[0, :, cols].astype(F32), kg_ref).astype(BF16)
        _store_vt(vt_ref, h, v_ref[0, :, cols].astype(F32))


def diff_prep(u, q_gain, k_gain, tables, q_col, k_col, v_col, ts):
    b, s, _ = u.shape
    w = DIFF_HEADS * LANES
    tab = pl.BlockSpec((ts, LANES), lambda i, j: (j, 0))
    vec = pl.BlockSpec((1, LANES), lambda i, j: (0, 0))
    vt_spec, vt_shape = _vt_spec_shape(b, DIFF_HEADS, s, ts, LANES)
    return pl.pallas_call(
        _diff_prep_body,
        grid=(b, s // ts),
        in_specs=[pl.BlockSpec((1, ts, w), lambda i, j: (i, j, q_col)),
                  pl.BlockSpec((1, ts, w), lambda i, j: (i, j, k_col)),
                  pl.BlockSpec((1, ts, w), lambda i, j: (i, j, v_col)),
                  vec, vec, tab, tab, tab],
        out_specs=[pl.BlockSpec((1, ts, 2 * w), lambda i, j: (i, j, 0)),
                   pl.BlockSpec((1, ts, w), lambda i, j: (i, j, 0)), vt_spec],
        out_shape=[jax.ShapeDtypeStruct((b, s, 2 * w), BF16), jax.ShapeDtypeStruct((b, s, w), BF16), vt_shape],
        compiler_params=_params("parallel", "parallel"),
        name="diff_prep",
    )(u, u, u, q_gain, k_gain, *tables)


def _gqa_prep_body(q_ref, k_ref, v_ref, qg_ref, kg_ref, c_ref, s1_ref, s2_ref, qo_ref, ko_ref, vt_ref):
    scale = HEAD_DIM ** -0.5 * LOG2E
    for h in range(GQA_Q_HEADS):
        cols = slice(h * LANES, (h + 1) * LANES)
        y = _rms(q_ref[0, :, cols].astype(F32), qg_ref[...])
        qo_ref[0, :, cols] = (_rope(y, c_ref, s1_ref, s2_ref, HEAD_DIM // 4) * scale).astype(BF16)
    for h in range(GQA_KV_HEADS):
        cols = slice(h * LANES, (h + 1) * LANES)
        y = _rms(k_ref[0, :, cols].astype(F32), kg_ref[...])
        ko_ref[0, :, cols] = _rope(y, c_ref, s1_ref, s2_ref, HEAD_DIM // 4).astype(BF16)
        _store_vt(vt_ref, h, v_ref[0, :, cols].astype(F32))


def gqa_prep(u, q_gain, k_gain, tables, q_col, k_col, v_col, ts):
    b, s, _ = u.shape
    wq = GQA_Q_HEADS * HEAD_DIM
    wk = GQA_KV_HEADS * HEAD_DIM
    tab = pl.BlockSpec((ts, LANES), lambda i, j: (j, 0))
    vec = pl.BlockSpec((1, LANES), lambda i, j: (0, 0))
    vt_spec, vt_shape = _vt_spec_shape(b, GQA_KV_HEADS, s, ts, HEAD_DIM)
    return pl.pallas_call(
        _gqa_prep_body,
        grid=(b, s // ts),
        in_specs=[pl.BlockSpec((1, ts, wq), lambda i, j: (i, j, q_col)),
                  pl.BlockSpec((1, ts, wk), lambda i, j: (i, j, k_col)),
                  pl.BlockSpec((1, ts, wk), lambda i, j: (i, j, v_col)),
                  vec, vec, tab, tab, tab],
        out_specs=[pl.BlockSpec((1, ts, wq), lambda i, j: (i, j, 0)),
                   pl.BlockSpec((1, ts, wk), lambda i, j: (i, j, 0)), vt_spec],
        out_shape=[jax.ShapeDtypeStruct((b, s, wq), BF16), jax.ShapeDtypeStruct((b, s, wk), BF16), vt_shape],
        compiler_params=_params("parallel", "parallel"),
        name="gqa_prep",
    )(u, u, u, q_gain, k_gain, *tables)


def _mla_prep_body(cq_ref, ckv_ref, ckr_ref, cqg_ref, ckvg_ref, wq_ref, wk_ref, wr_ref, wv_ref,
                   qg_ref, kg_ref, c_ref, s1_ref, s2_ref, qo_ref, ko_ref, vt_ref):
    scale = MLA_QK_DIM ** -0.5 * LOG2E
    cq = _rms(cq_ref[0].astype(F32), cqg_ref[...]).astype(BF16)
    ckv = _rms(ckv_ref[0].astype(F32), ckvg_ref[...]).astype(BF16)
    q = _dot(cq, wq_ref[...])
    k = _dot(ckv, wk_ref[...]) + _dot(ckr_ref[0], wr_ref[...])
    v = _dot(ckv, wv_ref[...])

    def norm_rope(x, g_ref):
        ms = jnp.sum(x * x, axis=-1, keepdims=True) * (1.0 / MLA_QK_DIM)
        y = x * lax.rsqrt(ms + EPS) * g_ref[...]
        return _rope(y, c_ref, s1_ref, s2_ref, MLA_ROPE_DIM // 2)

    for h in range(MLA_HEADS):
        cols = slice(h * MLA_PAD_DIM, (h + 1) * MLA_PAD_DIM)
        qo_ref[0, :, cols] = (norm_rope(q[:, cols], qg_ref) * scale).astype(BF16)
        ko_ref[0, :, cols] = norm_rope(k[:, cols], kg_ref).astype(BF16)
        _store_vt(vt_ref, h, v[:, h * MLA_V_DIM:(h + 1) * MLA_V_DIM])


def mla_prep(u, cq_gain, ckv_gain, w_q, w_k, w_r, w_v, q_gain, k_gain, tables, cq_col, ckv_col, ckr_col, ts):
    b, s, _ = u.shape
    q_rank, kv_rank = w_q.shape[0], w_k.shape[0]
    wqk = MLA_HEADS * MLA_PAD_DIM
    tab = pl.BlockSpec((ts, MLA_PAD_DIM), lambda i, j: (j, 0))
    vt_spec, vt_shape = _vt_spec_shape(b, MLA_HEADS, s, ts, MLA_V_DIM)

    def full(a):
        return pl.BlockSpec(a.shape, lambda i, j: (0,) * a.ndim)

    return pl.pallas_call(
        _mla_prep_body,
        grid=(b, s // ts),
        in_specs=[pl.BlockSpec((1, ts, q_rank), lambda i, j: (i, j, cq_col)),
                  pl.BlockSpec((1, ts, kv_rank), lambda i, j: (i, j, ckv_col)),
                  pl.BlockSpec((1, ts, LANES), lambda i, j: (i, j, ckr_col)),
                  full(cq_gain), full(ckv_gain), full(w_q), full(w_k), full(w_r), full(w_v),
                  full(q_gain), full(k_gain), tab, tab, tab],
        out_specs=[pl.BlockSpec((1, ts, wqk), lambda i, j: (i, j, 0)),
                   pl.BlockSpec((1, ts, wqk), lambda i, j: (i, j, 0)), vt_spec],
        out_shape=[jax.ShapeDtypeStruct((b, s, wqk), BF16), jax.ShapeDtypeStruct((b, s, wqk), BF16), vt_shape],
        compiler_params=_params("parallel", "parallel"),
        name="mla_prep",
    )(u, u, u, cq_gain, ckv_gain, w_q, w_k, w_r, w_v, q_gain, k_gain, *tables)


def _flash_body(q_ref, qn_ref, k_ref, vt_ref, o_ref, sa_scr, sb_scr, acc_scr, *, tk, n_split):
    nk = k_ref.shape[1] // tk
    tq = q_ref.shape[1]
    dv = o_ref.shape[2]
    hw = tq // n_split

    def scores(c, queries=q_ref):
        k = k_ref[0, pl.ds(pl.multiple_of(c * tk, tk), tk), :]
        return lax.dot_general(k, queries[0], (((1,), (1,)), ((), ())), preferred_element_type=F32)

    def softmax_pv(c, s_ref, m):
        m_out = []
        for h in range(n_split):
            cols = slice(h * hw, (h + 1) * hw)
            s = s_ref[:, cols]
            m_new = jnp.maximum(m[h], jnp.max(s, axis=0, keepdims=True))
            alpha = jnp.exp2(m[h] - m_new)
            p = jnp.exp2(s - m_new).astype(BF16)
            acc_scr[:, cols] = alpha * acc_scr[:, cols] + _dot(vt_ref[0, 0, c], p)
            m_out.append(m_new)
        return tuple(m_out)

    def pair(j, m):
        c = 2 * j
        sb_scr[...] = scores(c + 1)
        m = softmax_pv(c, sa_scr, m)
        sa_scr[...] = scores(c + 2)
        return softmax_pv(c + 1, sb_scr, m)

    acc_scr[...] = jnp.zeros(acc_scr.shape, F32)

    @pl.when(pl.program_id(2) == 0)
    def _():
        sa_scr[...] = scores(0)

    m = tuple(jnp.full((1, hw), -jnp.inf, F32) for _ in range(n_split))
    m = lax.fori_loop(0, nk // 2 - 1, pair, m, unroll=3)
    sb_scr[...] = scores(nk - 1)
    m = softmax_pv(nk - 2, sa_scr, m)
    sa_scr[...] = scores(0, qn_ref)
    softmax_pv(nk - 1, sb_scr, m)
    acc = acc_scr[...]
    o_ref[0] = (acc[0:dv, :] / acc[dv:dv + 1, :]).T.astype(o_ref.dtype)


def flash_attention(q, k, vt, n_maps, dk, q_col, k_col, v_head, out_dtype, tq):
    b, s, _ = q.shape
    _, _, nk, dva, tk = vt.shape
    dv = dva - VT_PAD
    nq = s // tq
    return pl.pallas_call(
        functools.partial(_flash_body, tk=tk, n_split=max(1, tq // MXU_WIDTH)),
        grid=(b, n_maps, nq),
        in_specs=[pl.BlockSpec((1, tq, dk), lambda i, h, a: (i, a, q_col(h))),
                  pl.BlockSpec((1, tq, dk), lambda i, h, a: (i, jnp.minimum(a + 1, nq - 1), q_col(h))),
                  pl.BlockSpec((1, s, dk), lambda i, h, a: (i, 0, k_col(h))),
                  pl.BlockSpec((1, 1, nk, dva, tk), lambda i, h, a: (i, v_head(h), 0, 0, 0))],
        out_specs=pl.BlockSpec((1, tq, dv), lambda i, h, a: (i, a, h)),
        out_shape=jax.ShapeDtypeStruct((b, s, n_maps * dv), out_dtype),
        scratch_shapes=[pltpu.VMEM((tk, tq), F32), pltpu.VMEM((tk, tq), F32), pltpu.VMEM((dva, tq), F32)],
        compiler_params=_params("arbitrary", "arbitrary", "arbitrary"),
        name="flash_attention",
    )(q, q, k, vt)


def _diff_combine_body(o_ref, lam_ref, g_ref, y_ref, *, lambda_init):
    lam = lam_ref[...]
    lam_full = (jnp.exp(jnp.sum(lam[0:1] * lam[1:2], keepdims=True))
                - jnp.exp(jnp.sum(lam[2:3] * lam[3:4], keepdims=True)) + lambda_init)
    w = DIFF_HEADS * LANES
    for h in range(DIFF_HEADS):
        cols = slice(h * LANES, (h + 1) * LANES)
        d = o_ref[0, :, cols] - lam_full * o_ref[0, :, w + h * LANES:w + (h + 1) * LANES]
        y_ref[0, :, cols] = (_rms(d, g_ref[...]) * (1.0 - lambda_init)).astype(y_ref.dtype)


def diff_combine(o, lam, out_gain, lambda_init, ts):
    b, s, w2 = o.shape
    w = w2 // 2
    return pl.pallas_call(
        functools.partial(_diff_combine_body, lambda_init=lambda_init),
        grid=(b, s // ts),
        in_specs=[pl.BlockSpec((1, ts, w2), lambda i, j: (i, j, 0)),
                  pl.BlockSpec(lam.shape, lambda i, j: (0, 0)),
                  pl.BlockSpec((1, LANES), lambda i, j: (0, 0))],
        out_specs=pl.BlockSpec((1, ts, w), lambda i, j: (i, j, 0)),
        out_shape=jax.ShapeDtypeStruct((b, s, w), BF16),
        compiler_params=_params("parallel", "parallel"),
        name="diff_combine",
    )(o, lam, out_gain)


def _merge_body(xn_ref, ya_ref, yb_ref, yc_ref, yd_ref, wg0, wg1, wg2, wg3, p0, p1, p2, p3, o_ref):
    xn = xn_ref[...]
    acc = None
    for y_ref, wg_ref, p_ref in ((ya_ref, wg0, p0), (yb_ref, wg1, p1), (yc_ref, wg2, p2), (yd_ref, wg3, p3)):
        term = _sigmoid(_dot(xn, wg_ref[...])) * _dot(y_ref[...], p_ref[0])
        acc = term if acc is None else acc + term
    o_ref[...] = acc.astype(o_ref.dtype)


def gated_merge(xn, ys, w_gate, w_branch, tm, tn):
    m, d = xn.shape
    width = w_branch.shape[1]
    nj = d // tn
    y_spec = pl.BlockSpec((tm, width), lambda i, j: (i, 0))
    wg_specs = [pl.BlockSpec((d, tn), functools.partial(lambda i, j, br: (0, br * nj + j), br=br)) for br in range(4)]
    p_specs = [pl.BlockSpec((1, width, tn), functools.partial(lambda i, j, br: (br, 0, j), br=br)) for br in range(4)]
    return pl.pallas_call(
        _merge_body,
        grid=(m // tm, nj),
        in_specs=[pl.BlockSpec((tm, d), lambda i, j: (i, 0)), y_spec, y_spec, y_spec, y_spec] + wg_specs + p_specs,
        out_specs=pl.BlockSpec((tm, tn), lambda i, j: (i, j)),
        out_shape=jax.ShapeDtypeStruct((m, d), BF16),
        compiler_params=_params("parallel", "arbitrary"),
        name="gated_merge",
    )(xn, *ys, w_gate, w_gate, w_gate, w_gate, w_branch, w_branch, w_branch, w_branch)


def _router_body(x_ref, g_ref, r_ref, idx_ref, w_ref, *, n_experts):
    xn = _rms(x_ref[...], g_ref[...])
    logits = jnp.dot(xn, r_ref[...], precision=lax.Precision.HIGHEST, preferred_element_type=F32)
    lane = lax.broadcasted_iota(jnp.int32, logits.shape, 1).astype(F32)
    neg = -jnp.inf
    lg = jnp.where(lane < n_experts, logits, neg)
    v1 = jnp.max(lg, axis=-1, keepdims=True)
    i1 = jnp.min(jnp.where(lg == v1, lane, float(LANES)), axis=-1, keepdims=True)
    lg2 = jnp.where(lane == i1, neg, lg)
    v2 = jnp.max(lg2, axis=-1, keepdims=True)
    i2 = jnp.min(jnp.where(lg2 == v2, lane, float(LANES)), axis=-1, keepdims=True)
    e21 = jnp.exp(v2 - v1)
    w1 = 1.0 / (1.0 + e21)
    w2 = e21 * w1
    idx_ref[...] = jnp.where(lane == 0.0, i1, jnp.where(lane == 1.0, i2, 0.0)).astype(jnp.int32)
    w_ref[...] = jnp.where(lane == 0.0, w1, jnp.where(lane == 1.0, w2, 0.0))


def moe_router(x, g, router_padded, n_experts, tm):
    m, d = x.shape
    return pl.pallas_call(
        functools.partial(_router_body, n_experts=n_experts),
        grid=(m // tm,),
        in_specs=[pl.BlockSpec((tm, d), lambda i: (i, 0)),
                  pl.BlockSpec((1, d), lambda i: (0, 0)),
                  pl.BlockSpec((d, LANES), lambda i: (0, 0))],
        out_specs=[pl.BlockSpec((tm, LANES), lambda i: (i, 0)), pl.BlockSpec((tm, LANES), lambda i: (i, 0))],
        out_shape=[jax.ShapeDtypeStruct((m, LANES), jnp.int32), jax.ShapeDtypeStruct((m, LANES), F32)],
        compiler_params=_params("parallel"),
        name="moe_router",
    )(x, g, router_padded)


def _moe_up_body(te_ref, nv_ref, tok_ref, x_hbm, g_ref, wg_ref, wu_ref, o_ref, xbuf, xn_scr, sem, *, rows_per_step):
    t = pl.program_id(0)
    f = pl.program_id(1)
    tm = xn_scr.shape[0]
    nt = pl.num_programs(0)
    nf = pl.num_programs(1)
    n_rows = rows_per_step * nf
    slot = t % 2

    def row_copy(dst_slot, r, tok):
        return pltpu.make_async_copy(x_hbm.at[pl.ds(tok, 1)], xbuf.at[dst_slot, pl.ds(r, 1)], sem.at[dst_slot])

    def prefetch_next_tile():
        for i in range(rows_per_step):
            r = f * rows_per_step + i
            row_copy(1 - slot, r, tok_ref[(t + 1) * tm + r]).start()

    def wait_rows(dst_slot):
        def wait(r, c):
            row_copy(dst_slot, r, 0).wait()
            return c
        lax.fori_loop(0, n_rows, wait, 0)

    @pl.when(jnp.logical_and(t == 0, f == 0))
    def _():
        def start(r, c):
            row_copy(0, r, tok_ref[r]).start()
            return c
        lax.fori_loop(0, n_rows, start, 0)

    @pl.when(f == 0)
    def _():
        wait_rows(slot)

    @pl.when(t < nv_ref[0])
    def _():
        @pl.when(f == 0)
        def _():
            xn_scr[...] = _rms(xbuf[slot, 0:tm, :], g_ref[...]).astype(BF16)

        prefetch_next_tile()
        _swiglu_columns(xn_scr[...], wg_ref.at[0], wu_ref.at[0], o_ref)

    @pl.when(t >= nv_ref[0])
    def _():
        prefetch_next_tile()
        o_ref[...] = jnp.zeros(o_ref.shape, o_ref.dtype)

    @pl.when(jnp.logical_and(t == nt - 1, f == nf - 1))
    def _():
        wait_rows(1 - slot)


def moe_up(tile_expert, n_valid, row_token, x, g, w_gate_up, tm, tf):
    d = x.shape[1]
    f = w_gate_up.shape[2] // 2
    nf = f // tf
    p = row_token.shape[0] - 2 * tm
    rows_per_step = -(-tm // nf)
    buf_rows = -(-rows_per_step * nf // 8) * 8

    def wmap(off, t, j, te, nv, tok):
        return (te[t], 0, off + jnp.where(t < nv[0], j, nf - 1))

    grid_spec = pltpu.PrefetchScalarGridSpec(
        num_scalar_prefetch=3,
        grid=(p // tm, nf),
        in_specs=[pl.BlockSpec(memory_space=pl.ANY),
                  pl.BlockSpec((1, d), lambda t, j, te, nv, tok: (0, 0)),
                  pl.BlockSpec((1, d, tf), functools.partial(wmap, 0)),
                  pl.BlockSpec((1, d, tf), functools.partial(wmap, nf))],
        out_specs=pl.BlockSpec((tm, tf), lambda t, j, te, nv, tok: (t, j)),
        scratch_shapes=[pltpu.VMEM((2, buf_rows, d), F32), pltpu.VMEM((tm, d), BF16), pltpu.SemaphoreType.DMA((2,))],
    )
    return pl.pallas_call(
        functools.partial(_moe_up_body, rows_per_step=rows_per_step),
        grid_spec=grid_spec,
        out_shape=jax.ShapeDtypeStruct((p, f), BF16),
        compiler_params=_params("arbitrary", "arbitrary"),
        name="moe_up",
    )(tile_expert, n_valid, row_token, x, g, w_gate_up, w_gate_up)


def _moe_down_body(te_ref, nv_ref, h_ref, w_ref, o_ref):
    t = pl.program_id(0)

    @pl.when(t < nv_ref[0])
    def _():
        o_ref[...] = _dot(h_ref[...], w_ref[0])

    @pl.when(t >= nv_ref[0])
    def _():
        o_ref[...] = jnp.zeros(o_ref.shape, o_ref.dtype)


def moe_down(tile_expert, n_valid, h, w_down, tm, tn):
    p, f = h.shape
    d = w_down.shape[2]
    nj = d // tn
    grid_spec = pltpu.PrefetchScalarGridSpec(
        num_scalar_prefetch=2,
        grid=(p // tm, nj),
        in_specs=[pl.BlockSpec((tm, f), lambda t, j, te, nv: (t, 0)),
                  pl.BlockSpec((1, f, tn), lambda t, j, te, nv: (te[t], 0, jnp.where(t < nv[0], j, nj - 1)))],
        out_specs=pl.BlockSpec((tm, tn), lambda t, j, te, nv: (t, j)),
    )
    return pl.pallas_call(
        _moe_down_body,
        grid_spec=grid_spec,
        out_shape=jax.ShapeDtypeStruct((p, d), F32),
        compiler_params=_params("arbitrary", "arbitrary"),
        name="moe_down",
    )(tile_expert, n_valid, h, w_down)


def _moe_combine_body(pos_ref, x_ref, w_ref, y_hbm, o_ref, buf0, buf1, sem):
    i = pl.program_id(0)
    tc = buf0.shape[0]

    def row_copy(r, k, src):
        dst = buf0 if k == 0 else buf1
        return pltpu.make_async_copy(y_hbm.at[pl.ds(src, 1)], dst.at[pl.ds(r, 1)], sem)

    def start(r, c):
        for k in range(TOP_K):
            row_copy(r, k, pos_ref[(i * tc + r) * TOP_K + k]).start()
        return c

    def wait(r, c):
        for k in range(TOP_K):
            row_copy(r, k, 0).wait()
        return c

    lax.fori_loop(0, tc, start, 0, unroll=8)
    lax.fori_loop(0, tc, wait, 0, unroll=8)
    o_ref[...] = x_ref[...] + w_ref[:, 0:1] * buf0[...] + w_ref[:, 1:2] * buf1[...]


def moe_combine(pos, x, wts, y, tc):
    m, d = x.shape
    grid_spec = pltpu.PrefetchScalarGridSpec(
        num_scalar_prefetch=1,
        grid=(m // tc,),
        in_specs=[pl.BlockSpec((tc, d), lambda i, pos: (i, 0)),
                  pl.BlockSpec((tc, LANES), lambda i, pos: (i, 0)),
                  pl.BlockSpec(memory_space=pl.ANY)],
        out_specs=pl.BlockSpec((tc, d), lambda i, pos: (i, 0)),
        scratch_shapes=[pltpu.VMEM((tc, d), F32), pltpu.VMEM((tc, d), F32), pltpu.SemaphoreType.DMA(())],
    )
    return pl.pallas_call(
        _moe_combine_body,
        grid_spec=grid_spec,
        out_shape=jax.ShapeDtypeStruct((m, d), F32),
        compiler_params=_params("arbitrary"),
        name="moe_combine",
    )(pos, x, wts, y)


def _moe_schedule(idx, n_experts, tm):
    m = idx.shape[0]
    n_assign = m * TOP_K
    e_flat = idx.reshape(n_assign)
    onehot = (e_flat[:, None] == jnp.arange(n_experts, dtype=jnp.int32)[None, :]).astype(jnp.int32)
    csum = jnp.cumsum(onehot, axis=0)
    rank = jnp.take_along_axis(csum, e_flat[:, None], axis=1)[:, 0] - 1
    counts = csum[-1]
    padded = ((counts + tm - 1) // tm) * tm
    ends = jnp.cumsum(padded)
    pos = (ends - padded)[e_flat] + rank
    n_rows = n_assign + n_experts * tm
    row_token = jnp.zeros((n_rows + 2 * tm,), jnp.int32).at[pos].set(jnp.arange(n_assign, dtype=jnp.int32) // TOP_K)
    n_tiles = n_rows // tm
    tiles = jnp.arange(n_tiles, dtype=jnp.int32)
    n_valid = (ends[-1] // tm).astype(jnp.int32)
    tile_expert = jnp.sum((tiles[:, None] * tm >= ends[None, :]).astype(jnp.int32), axis=1)
    tile_expert = jnp.minimum(tile_expert, n_experts - 1)
    tile_expert = jnp.where(tiles < n_valid, tile_expert, tile_expert[n_valid - 1])
    return pos.astype(jnp.int32), row_token, tile_expert, n_valid.reshape(1)


def moe_ffn(x, g, router, w_gate_up, w_down):
    m, d = x.shape
    n_experts = router.shape[1]
    tm = _tile(m, 512)
    router_padded = jnp.pad(router, ((0, 0), (0, LANES - n_experts)))
    idx, wts = moe_router(x, g, router_padded, n_experts, _tile(m, 512))
    pos, row_token, tile_expert, n_valid = _moe_schedule(idx[:, :TOP_K], n_experts, tm)
    f = w_down.shape[1]
    tf = next((c for c in (7 * MXU_WIDTH, 4 * MXU_WIDTH, 2 * MXU_WIDTH) if f % c == 0), f)
    h = moe_up(tile_expert, n_valid, row_token, x, g, w_gate_up, tm, tf)
    y = moe_down(tile_expert, n_valid, h, w_down, tm, _tile(d, 512))
    return moe_combine(pos, x, wts, y, _tile(m, 256))


def _rope_cos_sin(pos, dim, theta):
    inv = jnp.power(jnp.float32(theta), -jnp.arange(0, dim, 2, dtype=F32) / dim)
    ang = pos.astype(F32)[:, None] * inv[None, :]
    return jnp.cos(ang), jnp.sin(ang)


def _rope_tables(s):
    pos = jnp.arange(s, dtype=jnp.int32)
    rows = s // GRID_W
    row_pos = jnp.broadcast_to(jnp.arange(rows, dtype=jnp.int32)[:, None], (rows, GRID_W)).reshape(-1)
    col_pos = jnp.broadcast_to(jnp.arange(GRID_W, dtype=jnp.int32)[None, :], (rows, GRID_W)).reshape(-1)

    def ones(n):
        return jnp.ones((s, n), F32)

    def zeros(n):
        return jnp.zeros((s, n), F32)

    cat = functools.partial(jnp.concatenate, axis=1)
    cos, sin = _rope_cos_sin(pos, DIFF_ROPE_DIM, ROPE_THETA)
    rest = DIFF_QK_DIM - DIFF_ROPE_DIM
    half = DIFF_ROPE_DIM // 2
    diff = (cat([cos, cos, ones(rest)] * 2), cat([-sin, zeros(half), zeros(rest)] * 2),
            cat([zeros(half), sin, zeros(rest)] * 2))
    cos, sin = _rope_cos_sin(pos, MLA_ROPE_DIM, ROPE_THETA)
    half = MLA_ROPE_DIM // 2
    tail = MLA_PAD_DIM - MLA_QK_DIM
    mla = (cat([ones(MLA_NOPE_DIM), cos, cos, ones(tail)]),
           cat([zeros(MLA_NOPE_DIM), -sin, zeros(half), zeros(tail)]),
           cat([zeros(MLA_NOPE_DIM), zeros(half), sin, zeros(tail)]))
    cos_r, sin_r = _rope_cos_sin(row_pos, HEAD_DIM // 2, AXIAL_THETA)
    cos_c, sin_c = _rope_cos_sin(col_pos, HEAD_DIM // 2, AXIAL_THETA)
    half = HEAD_DIM // 4
    gqa = (cat([cos_r, cos_r, cos_c, cos_c]), cat([-sin_r, zeros(half), -sin_c, zeros(half)]),
           cat([zeros(half), sin_r, zeros(half), sin_c]))
    return diff, mla, gqa


COL_A_GATE, COL_A_X, COL_B_Q, COL_B_K, COL_B_V, COL_C_Q, COL_D_Q = (i * 512 for i in range(7))
COL_C_KV, COL_D_K, COL_D_V, COL_C_KR = 3584, 3840, 4096, 4352
MIX_COLS = 4608
MIX_TN = 3 * MXU_WIDTH


def _mixer_in_weights(w_in_l):
    widths = (512, 512, 512, 512, 512, 512, 256, 64, 512, 256, 256)
    offs = [0]
    for wd in widths:
        offs.append(offs[-1] + wd)
    (a_gate, a_x, b_q, b_k, b_v, c_q, c_kv, c_kr, d_q, d_k, d_v) = (
        w_in_l[:, offs[i]:offs[i + 1]] for i in range(len(widths)))
    pad = jnp.zeros((w_in_l.shape[0], MIX_COLS - COL_C_KR - MLA_ROPE_DIM), w_in_l.dtype)
    w_mix = jnp.concatenate([a_gate, a_x, b_q, b_k, b_v, c_q, d_q, c_kv, d_k, d_v, c_kr, pad], axis=1)
    return w_mix.astype(BF16), w_in_l[:, offs[-1]:].astype(BF16)


def _mla_weights(w_uq, w_ukv):
    q_rank, kv_rank = w_uq.shape[0], w_ukv.shape[0]
    wq = w_uq.reshape(q_rank, MLA_HEADS, MLA_QK_DIM)
    wq = jnp.pad(wq, ((0, 0), (0, 0), (0, MLA_PAD_DIM - MLA_QK_DIM))).reshape(q_rank, MLA_HEADS * MLA_PAD_DIM)
    wkv = w_ukv.reshape(kv_rank, MLA_HEADS, MLA_NOPE_DIM + MLA_V_DIM)
    wk = jnp.pad(wkv[:, :, :MLA_NOPE_DIM], ((0, 0), (0, 0), (0, MLA_PAD_DIM - MLA_NOPE_DIM)))
    wk = wk.reshape(kv_rank, MLA_HEADS * MLA_PAD_DIM)
    wv = wkv[:, :, MLA_NOPE_DIM:].reshape(kv_rank, MLA_HEADS * MLA_V_DIM)
    place = jnp.zeros((LANES, MLA_HEADS, MLA_PAD_DIM), F32)
    r = jnp.arange(MLA_ROPE_DIM)
    place = place.at[r, :, MLA_NOPE_DIM + r].set(1.0).reshape(LANES, MLA_HEADS * MLA_PAD_DIM)
    return wq.astype(BF16), wk.astype(BF16), place.astype(BF16), wv.astype(BF16)


def _lru_gate_weights(gate_w):
    nb = gate_w.shape[2]
    per = LANES // LRU_BLOCK_DIM
    w = gate_w.reshape(4, nb // per, per, LRU_BLOCK_DIM, LRU_BLOCK_DIM)
    bd = jnp.einsum("dcpkj,pq->cdpkqj", w, jnp.eye(per, dtype=w.dtype))
    return bd.reshape(nb // per, 4, LANES, LANES).astype(BF16)


def _pad_gain(g, width):
    return jnp.pad(g, (0, width - g.shape[0])).reshape(1, width)


def kernel(x, norm_mix, w_in, lru_conv_w, lru_conv_b, lru_gate_w, lru_gate_b, lru_lambda, diff_q_norm, diff_k_norm, diff_lambda, diff_out_norm, mla_cq_norm, mla_ckv_norm, mla_w_uq, mla_w_ukv, mla_q_norm, mla_k_norm, gqa_q_norm, gqa_k_norm, w_branch, w_out, norm_ffn, ffn_w_gate_up, ffn_w_down, moe_router, moe_w_gate_up, moe_w_down):
    b, s, d = x.shape
    m = b * s
    depth = w_in.shape[0]
    lru_width = lru_conv_w.shape[2]
    tables_diff, tables_mla, tables_gqa = _rope_tables(s)
    ts = 512 if s % 1024 == 0 else s // 2
    tq = _tile(s, 1024)
    tm = _tile(m, 512)
    tm_big = _tile(m, 1024)
    xf = x.reshape(m, d)

    for l in range(depth):
        w_mix, w_gate = _mixer_in_weights(w_in[l])
        u, xn = norm_proj(xf, norm_mix[l].reshape(1, d), w_mix, tm_big, MIX_TN)
        u = u.reshape(b, s, MIX_COLS)

        y_a = lru_mixer(u, lru_conv_w[l], lru_conv_b[l].reshape(1, lru_width), _lru_gate_weights(lru_gate_w[l]),
                        lru_gate_b[l].reshape(4, lru_width), lru_lambda[l], COL_A_GATE // LANES, COL_A_X // LANES)

        lambda_init = 0.8 - 0.6 * math.exp(-0.3 * l)
        q_b, k_b, vt_b = diff_prep(u, jnp.tile(diff_q_norm[l], 2).reshape(1, LANES),
                                   jnp.tile(diff_k_norm[l], 2).reshape(1, LANES),
                                   tables_diff, COL_B_Q // 512, COL_B_K // 512, COL_B_V // 512, ts)
        o_b = flash_attention(q_b, k_b, vt_b, 2 * DIFF_HEADS, LANES,
                              lambda h: h, lambda h: h % DIFF_HEADS, lambda h: h % DIFF_HEADS, F32, tq)
        y_b = diff_combine(o_b, diff_lambda[l], diff_out_norm[l].reshape(1, LANES), lambda_init, ts)

        wq, wk, wr, wv = _mla_weights(mla_w_uq[l], mla_w_ukv[l])
        q_c, k_c, vt_c = mla_prep(u, mla_cq_norm[l].reshape(1, -1), mla_ckv_norm[l].reshape(1, -1), wq, wk, wr, wv,
                                 _pad_gain(mla_q_norm[l], MLA_PAD_DIM), _pad_gain(mla_k_norm[l], MLA_PAD_DIM),
                                 tables_mla, COL_C_Q // 512, COL_C_KV // 256, COL_C_KR // LANES, ts)
        y_c = flash_attention(q_c, k_c, vt_c, MLA_HEADS, MLA_PAD_DIM,
                              lambda h: h, lambda h: h, lambda h: h, BF16, tq)

        group = GQA_Q_HEADS // GQA_KV_HEADS
        q_d, k_d, vt_d = gqa_prep(u, gqa_q_norm[l].reshape(1, LANES), gqa_k_norm[l].reshape(1, LANES), tables_gqa,
                                  COL_D_Q // 512, COL_D_K // 256, COL_D_V // 256, ts)
        y_d = flash_attention(q_d, k_d, vt_d, GQA_Q_HEADS, HEAD_DIM,
                              lambda h: h, lambda h: h // group, lambda h: h // group, BF16, tq)

        ys = [y.reshape(m, -1) for y in (y_a, y_b, y_c, y_d)]
        merged = gated_merge(xn, ys, w_gate, w_branch[l].astype(BF16), tm, _tile(d, 512))
        xf = mm_residual(merged, w_out[l].astype(BF16), xf, tm_big, _tile(d, 512))

        g_ffn = norm_ffn[l].reshape(1, d)
        if l % 2 == 0:
            f = ffn_w_down.shape[1]
            act = norm_swiglu(xf, g_ffn, ffn_w_gate_up[l // 2].astype(BF16), tm_big, _tile(f, 512))
            xf = mm_residual(act, ffn_w_down[l // 2].astype(BF16), xf, tm_big, _tile(d, 512))
        else:
            xf = moe_ffn(xf, g_ffn, moe_router[l // 2], moe_w_gate_up[l // 2].astype(BF16),
                         moe_w_down[l // 2].astype(BF16))
    return xf.reshape(b, s, d)
```
